```python
import math
import jax
import jax.numpy as jnp
from jax import lax
import numpy as np

D_MODEL = 1024
BATCH = 16
SEQ = 2048
DEPTH = 2
DEC_BATCH = 1
DEC_SEQ = 16384
PAST_LEN = 128

GRID_W = 64
N_MIXERS = 2
N_RET_LAYERS = (DEPTH + 1) // 2
N_NA_LAYERS = DEPTH // 2
RET_HEADS = 4
RET_DK = 256
RET_DV = 512
RET_QK_W = RET_HEADS * RET_DK
RET_V_W = RET_HEADS * RET_DV
RET_CHUNK = 128
ROPE_BASE = 10000.0
NA_HEADS = 32
NA_HD = D_MODEL // NA_HEADS
NA_KH = 8
NA_KW = 16
N_EXPERTS = 16
EXPERT_FF = 1024
CAPACITY_FACTOR = 2
DN_ALPHA = (2 * DEPTH) ** 0.25
DN_BETA = (8 * DEPTH) ** -0.25
LN_EPS = 1e-5
GN_EPS = 1e-6

kernel_name = "hybrid_retention_natten_ecmoe_encoder"


def layer_norm(x, g, b):
    xf = x.astype(jnp.float32)
    mu = jnp.mean(xf, axis=-1, keepdims=True)
    var = jnp.mean(jnp.square(xf - mu), axis=-1, keepdims=True)
    y = (xf - mu) * lax.rsqrt(var + LN_EPS)
    return (y * g.astype(jnp.float32) + b.astype(jnp.float32)).astype(x.dtype)


def head_group_norm(y):
    yf = y.astype(jnp.float32)
    mu = jnp.mean(yf, axis=-1, keepdims=True)
    var = jnp.mean(jnp.square(yf - mu), axis=-1, keepdims=True)
    return ((yf - mu) * lax.rsqrt(var + GN_EPS)).astype(y.dtype)


def rotary_tables(S, dim):
    inv = 1.0 / (ROPE_BASE ** (jnp.arange(0, dim, 2, dtype=jnp.float32) / dim))
    ang = jnp.arange(S, dtype=jnp.float32)[:, None] * inv[None, :]
    return jnp.cos(ang), jnp.sin(ang)


def apply_rotary(x, cos, sin):
    half = x.shape[-1] // 2
    x1, x2 = x[..., :half], x[..., half:]
    c = cos[None, :, None, :].astype(x.dtype)
    s = sin[None, :, None, :].astype(x.dtype)
    return jnp.concatenate([x1 * c - x2 * s, x1 * s + x2 * c], axis=-1)


def chunk_retention(q, k, v, log_g, inclusive):
    B, S, H, DK = q.shape
    DV = v.shape[-1]
    C = RET_CHUNK
    NC = S // C
    dt = q.dtype
    i = jnp.arange(C, dtype=jnp.float32)
    diff = i[:, None] - i[None, :]
    mask = (diff >= 0) if inclusive else (diff > 0)
    dmask = jnp.where(mask[None], jnp.exp(log_g[:, None, None] * jnp.maximum(diff, 0.0)[None]), 0.0).astype(dt)
    cross_decay = jnp.exp(log_g[None, :] * (i[:, None] + 1.0)).astype(dt)
    state_decay = jnp.exp(log_g[None, :] * (C - 1.0 - i[:, None])).astype(dt)
    chunk_decay = jnp.exp(log_g * C).astype(dt)

    def to_chunks(a):
        return a.reshape(B, NC, C, H, a.shape[-1]).transpose(1, 0, 2, 3, 4)

    def step(R, inp):
        qc, kc, vc = inp
        scores = jnp.einsum('bihd,bjhd->bhij', qc, kc) * dmask[None]
        o = jnp.einsum('bhij,bjhe->bihe', scores, vc)
        o = o + jnp.einsum('bihd,bhde->bihe', qc, R) * cross_decay[None, :, :, None]
        R = R * chunk_decay[None, :, None, None] + jnp.einsum(
            'bjhd,bjhe->bhde', kc * state_decay[None, :, :, None], vc)
        return R, o

    R0 = jnp.zeros((B, H, DK, DV), dt)
    _, o = lax.scan(step, R0, (to_chunks(q), to_chunks(k), to_chunks(v)))
    return o.transpose(1, 0, 2, 3, 4).reshape(B, S, H, DV)


def retention(x, w_in, decay_logit, w_out):
    B, S, _ = x.shape
    proj = x @ w_in
    q, k, v, g = jnp.split(proj, [RET_QK_W, 2 * RET_QK_W, 2 * RET_QK_W + RET_V_W], axis=-1)
    q = q.reshape(B, S, RET_HEADS, RET_DK)
    k = k.reshape(B, S, RET_HEADS, RET_DK) * (RET_DK ** -0.5)
    v = v.reshape(B, S, RET_HEADS, RET_DV)
    cos, sin = rotary_tables(S, RET_DK)
    q = apply_rotary(q, cos, sin)
    k = apply_rotary(k, cos, sin)
    log_g = jax.nn.log_sigmoid(decay_logit.astype(jnp.float32))
    y_fwd = chunk_retention(q, k, v, log_g[0], True)
    y_bwd = jnp.flip(chunk_retention(jnp.flip(q, 1), jnp.flip(k, 1), jnp.flip(v, 1), log_g[1], False), 1)
    y = head_group_norm(y_fwd + y_bwd).reshape(B, S, RET_V_W)
    return (jax.nn.silu(g) * y) @ w_out


def neighbourhood_attention(x, w_in, rpb, w_out):
    B, S, D = x.shape
    rows = S // GRID_W
    kh = min(NA_KH, rows)
    qkv = (x @ w_in).reshape(B, rows, GRID_W, 3, NA_HEADS, NA_HD)
    q = qkv[:, :, :, 0] * (NA_HD ** -0.5)
    k = qkv[:, :, :, 1]
    v = qkv[:, :, :, 2]
    cols = jnp.arange(GRID_W)
    col_start = jnp.clip(cols - NA_KW // 2, 0, GRID_W - NA_KW)
    col_idx = col_start[:, None] + jnp.arange(NA_KW)[None, :]
    col_off = col_idx - cols[:, None] + (NA_KW - 1)
    bias_cols = rpb[:, :, col_off]

    def row_block(r):
        rs = jnp.clip(r - kh // 2, 0, rows - kh)
        k_rows = lax.dynamic_slice_in_dim(k, rs, kh, axis=1)
        v_rows = lax.dynamic_slice_in_dim(v, rs, kh, axis=1)
        k_g = k_rows[:, :, col_idx]
        v_g = v_rows[:, :, col_idx]
        q_r = lax.dynamic_index_in_dim(q, r, axis=1, keepdims=False)
        row_off = rs + jnp.arange(kh) - r + (NA_KH - 1)
        bias = bias_cols[:, row_off].transpose(0, 2, 1, 3)
        s = jnp.einsum('bwhd,bawkhd->bhwak', q_r, k_g) + bias[None].astype(q_r.dtype)
        p = jax.nn.softmax(s.reshape(B, NA_HEADS, GRID_W, kh * NA_KW).astype(jnp.float32), axis=-1)
        p = p.astype(v_g.dtype).reshape(B, NA_HEADS, GRID_W, kh, NA_KW)
        return jnp.einsum('bhwak,bawkhd->bwhd', p, v_g)

    o = lax.map(row_block, jnp.arange(rows))
    o = o.transpose(1, 0, 2, 3, 4).reshape(B, S, D)
    return o @ w_out


def expert_choice_ffn(x, w_router, w_gate, w_up, w_down):
    B, S, D = x.shape
    T = B * S
    cap = CAPACITY_FACTOR * T // N_EXPERTS
    xt = x.reshape(T, D)
    aff = jax.nn.softmax((xt @ w_router).astype(jnp.float32), axis=-1)
    gates, idx = lax.top_k(aff.T, cap)
    xe = xt[idx]
    h = jax.nn.silu(jnp.einsum('ecd,edf->ecf', xe, w_gate)) * jnp.einsum('ecd,edf->ecf', xe, w_up)
    ye = jnp.einsum('ecf,efd->ecd', h, w_down) * gates[..., None].astype(x.dtype)
    y = jnp.zeros((T, D), x.dtype).at[idx.reshape(-1)].add(ye.reshape(-1, D))
    return y.reshape(B, S, D)


def trunk(x, ret_w_in, ret_decay, ret_w_out, na_w_in, na_rpb, na_w_out,
          ln_mix_g, ln_mix_b, ln_ffn_g, ln_ffn_b, w_router, w_gate, w_up, w_down):
    for i in range(DEPTH):
        j = i // N_MIXERS
        if i % N_MIXERS == 0:
            h = retention(x, ret_w_in[j], ret_decay[j], ret_w_out[j])
        else:
            h = neighbourhood_attention(x, na_w_in[j], na_rpb[j], na_w_out[j])
        x = layer_norm(DN_ALPHA * x + h, ln_mix_g[i], ln_mix_b[i])
        f = expert_choice_ffn(x, w_router[i], w_gate[i], w_up[i], w_down[i])
        x = layer_norm(DN_ALPHA * x + f, ln_ffn_g[i], ln_ffn_b[i])
    return x


def _xavier(key, shape, fan_in, fan_out, scale=1.0):
    std = scale * math.sqrt(2.0 / (fan_in + fan_out))
    return jax.random.normal(key, shape, jnp.float32) * std


def setup_inputs(seed: int = 0) -> dict:
    key = jax.random.key(seed)
    ks = jax.random.split(key, 24)
    D = D_MODEL
    x_prompt = jax.random.normal(ks[0], (BATCH, SEQ, D), jnp.float32)
    x_sample = jax.random.normal(ks[1], (DEC_BATCH, DEC_SEQ, D), jnp.float32)
    r_q = jax.random.normal(ks[2], (N_RET_LAYERS, D, RET_QK_W), jnp.float32) * D ** -0.5
    r_k = jax.random.normal(ks[3], (N_RET_LAYERS, D, RET_QK_W), jnp.float32) * D ** -0.5
    r_v = _xavier(ks[4], (N_RET_LAYERS, D, RET_V_W), D, RET_V_W, DN_BETA)
    r_g = jax.random.normal(ks[5], (N_RET_LAYERS, D, RET_V_W), jnp.float32) * D ** -0.5
    ret_w_in = jnp.concatenate([r_q, r_k, r_v, r_g], axis=-1)
    g0 = 1.0 - 2.0 ** (-5.0 - np.arange(RET_HEADS))
    base_logit = jnp.asarray(np.log(g0 / (1.0 - g0)), jnp.float32)
    ret_decay = base_logit[None, None, :] + 0.1 * jax.random.normal(ks[6], (N_RET_LAYERS, 2, RET_HEADS), jnp.float32)
    ret_w_out = _xavier(ks[7], (N_RET_LAYERS, RET_V_W, D), RET_V_W, D, DN_BETA)
    n_qk = jax.random.normal(ks[8], (N_NA_LAYERS, D, 2 * D), jnp.float32) * D ** -0.5
    n_v = _xavier(ks[9], (N_NA_LAYERS, D, D), D, D, DN_BETA)
    na_w_in = jnp.concatenate([n_qk, n_v], axis=-1)
    na_rpb = 0.1 * jax.random.normal(ks[10], (N_NA_LAYERS, NA_HEADS, 2 * NA_KH - 1, 2 * NA_KW - 1), jnp.float32)
    na_w_out = _xavier(ks[11], (N_NA_LAYERS, D, D), D, D, DN_BETA)
    ln_mix_g = 1.0 + 0.02 * jax.random.normal(ks[12], (DEPTH, D), jnp.float32)
    ln_mix_b = 0.02 * jax.random.normal(ks[13], (DEPTH, D), jnp.float32)
    ln_ffn_g = 1.0 + 0.02 * jax.random.normal(ks[14], (DEPTH, D), jnp.float32)
    ln_ffn_b = 0.02 * jax.random.normal(ks[15], (DEPTH, D), jnp.float32)
    w_router = jax.random.normal(ks[16], (DEPTH, D, N_EXPERTS), jnp.float32) * D ** -0.5
    w_gate = _xavier(ks[17], (DEPTH, N_EXPERTS, D, EXPERT_FF), D, EXPERT_FF, DN_BETA)
    w_up = _xavier(ks[18], (DEPTH, N_EXPERTS, D, EXPERT_FF), D, EXPERT_FF, DN_BETA)
    w_down = _xavier(ks[19], (DEPTH, N_EXPERTS, EXPERT_FF, D), EXPERT_FF, D, DN_BETA)
    return {"x_prompt": x_prompt, "x_sample": x_sample,
            "ret_w_in": ret_w_in, "ret_decay": ret_decay, "ret_w_out": ret_w_out,
            "na_w_in": na_w_in, "na_rpb": na_rpb, "na_w_out": na_w_out,
            "ln_mix_g": ln_mix_g, "ln_mix_b": ln_mix_b, "ln_ffn_g": ln_ffn_g, "ln_ffn_b": ln_ffn_b,
            "w_router": w_router, "w_gate": w_gate, "w_up": w_up, "w_down": w_down}


def reference(x_prompt, x_sample, ret_w_in, ret_decay, ret_w_out, na_w_in, na_rpb, na_w_out,
              ln_mix_g, ln_mix_b, ln_ffn_g, ln_ffn_b, w_router, w_gate, w_up, w_down):
    y_prompt = trunk(x_prompt, ret_w_in, ret_decay, ret_w_out, na_w_in, na_rpb, na_w_out,
                     ln_mix_g, ln_mix_b, ln_ffn_g, ln_ffn_b, w_router, w_gate, w_up, w_down)
    y_sample = trunk(x_sample, ret_w_in, ret_decay, ret_w_out, na_w_in, na_rpb, na_w_out,
                     ln_mix_g, ln_mix_b, ln_ffn_g, ln_ffn_b, w_router, w_gate, w_up, w_down)
    return (y_prompt, y_sample)
```

```python
import functools

import jax
import jax.numpy as jnp
from jax import lax
from jax.experimental import pallas as pl
from jax.experimental.pallas import tpu as pltpu

F32 = jnp.float32
BF16 = jnp.bfloat16

D_MODEL = 1024
DEPTH = 2
GRID_W = 64
RET_HEADS = 4
RET_DK = 256
RET_DV = 512
RET_QK_W = RET_HEADS * RET_DK
RET_V_W = RET_HEADS * RET_DV
ROPE_BASE = 10000.0
NA_HEADS = 32
NA_HD = D_MODEL // NA_HEADS
NA_KH = 8
NA_KW = 16
N_EXPERTS = 16
EXPERT_FF = 1024
CAPACITY_FACTOR = 2
DN_ALPHA = (2 * DEPTH) ** 0.25
LN_EPS = 1e-5
GN_EPS = 1e-6

VMEM_LIMIT_BYTES = 56 * 1024 * 1024
RET_CHUNK = 256
NA_ROWS_PER_STEP = 8
NA_HEAD_GROUP = 8
NA_QGROUP = 16
NA_WIN_COLS = 32
NA_WIN_STARTS = (0, 8, 24, 32)
NEG_BIG = -1e30


def _cparams(sem):
    return pltpu.CompilerParams(dimension_semantics=sem, vmem_limit_bytes=VMEM_LIMIT_BYTES)


def _proj_kernel(x_ref, w_ref, cos_ref, sin_ref, o_ref, wbf_ref, *, mode):
    j = pl.program_id(0)
    i = pl.program_id(1)

    @pl.when(i == 0)
    def _():
        wbf_ref[...] = w_ref[...].astype(BF16)

    acc = jnp.dot(x_ref[...].astype(BF16), wbf_ref[...], preferred_element_type=F32)
    if mode == "na":
        o_ref[...] = acc * jnp.where(j == 0, NA_HD ** -0.5, 1.0).astype(F32)
    else:
        @pl.when(j >= 2)
        def _():
            o_ref[...] = acc

        @pl.when(j < 2)
        def _():
            scale = jnp.where(j == 1, RET_DK ** -0.5, 1.0).astype(F32)
            c = cos_ref[...]
            s = sin_ref[...]
            half = RET_DK // 2
            for h in range(RET_HEADS):
                x1 = acc[:, h * RET_DK:h * RET_DK + half] * scale
                x2 = acc[:, h * RET_DK + half:(h + 1) * RET_DK] * scale
                o_ref[:, h * RET_DK:h * RET_DK + half] = x1 * c - x2 * s
                o_ref[:, h * RET_DK + half:(h + 1) * RET_DK] = x1 * s + x2 * c


def _proj(x2d, w, cos, sin, *, mode, seq):
    T, K = x2d.shape
    N = w.shape[1]
    tm, tn = 512, 1024
    nseq = seq // tm
    return pl.pallas_call(
        functools.partial(_proj_kernel, mode=mode),
        grid=(N // tn, T // tm),
        in_specs=[
            pl.BlockSpec((tm, K), lambda j, i: (i, 0)),
            pl.BlockSpec((K, tn), lambda j, i: (0, j)),
            pl.BlockSpec((tm, RET_DK // 2), lambda j, i: (i % nseq, 0)),
            pl.BlockSpec((tm, RET_DK // 2), lambda j, i: (i % nseq, 0)),
        ],
        out_specs=pl.BlockSpec((tm, tn), lambda j, i: (i, j)),
        out_shape=jax.ShapeDtypeStruct((T, N), F32),
        scratch_shapes=[pltpu.VMEM((K, tn), BF16)],
        compiler_params=_cparams(("parallel", "arbitrary")),
        name="proj_" + mode,
    )(x2d, w, cos, sin)


def _rotary_tables(seq):
    inv = 1.0 / (ROPE_BASE ** (jnp.arange(0, RET_DK, 2, dtype=F32) / RET_DK))
    ang = jnp.arange(seq, dtype=F32)[:, None] * inv[None, :]
    return jnp.cos(ang), jnp.sin(ang)


def _ret_kernel(cdec_ref, q_ref, k_ref, v_ref, dm_ref, cd_ref, sd_ref, o_ref, state_ref):
    d = pl.program_id(0)
    h = pl.program_id(2)
    c = pl.program_id(3)

    @pl.when(c == 0)
    def _():
        state_ref[...] = jnp.zeros_like(state_ref)

    q = q_ref[...].astype(BF16)
    k = k_ref[...]
    v = v_ref[...].astype(BF16)
    scores = lax.dot_general(q, k.astype(BF16), (((1,), (1,)), ((), ())),
                             preferred_element_type=F32) * dm_ref[...]
    o = jnp.dot(scores.astype(BF16), v, preferred_element_type=F32)
    state = state_ref[...]
    o = o + jnp.dot(q, state.astype(BF16), preferred_element_type=F32) * cd_ref[...]
    ks = (k * sd_ref[...]).astype(BF16)
    state_ref[...] = state * cdec_ref[d * RET_HEADS + h] + lax.dot_general(
        ks, v, (((0,), (0,)), ((), ())), preferred_element_type=F32)
    o_ref[...] = o


def _ret_tables(decay_logit, C):
    log_g = jax.nn.log_sigmoid(decay_logit.astype(F32))
    i = jnp.arange(C, dtype=F32)
    diff = i[:, None] - i[None, :]
    lf = log_g[0][:, None, None]
    lb = log_g[1][:, None, None]
    dm_f = jnp.where((diff >= 0)[None], jnp.exp(lf * jnp.maximum(diff, 0.0)[None]), 0.0)
    dm_b = jnp.where((diff < 0)[None], jnp.exp(lb * jnp.maximum(-diff, 0.0)[None]), 0.0)
    cd_f = jnp.exp(log_g[0][:, None] * (i[None, :] + 1.0))
    cd_b = jnp.exp(log_g[1][:, None] * (C - i[None, :]))
    sd_f = jnp.exp(log_g[0][:, None] * (C - 1.0 - i[None, :]))
    sd_b = jnp.exp(log_g[1][:, None] * i[None, :])
    dm = jnp.stack([dm_f, dm_b]).astype(F32)
    cd = jnp.stack([cd_f, cd_b]).astype(F32)[..., None]
    sd = jnp.stack([sd_f, sd_b]).astype(F32)[..., None]
    cdec = jnp.exp(log_g * C).reshape(-1).astype(F32)
    return dm, cd, sd, cdec


def _retention_core(proj, decay_logit, *, batch, seq):
    T = proj.shape[0]
    C = RET_CHUNK
    NC = seq // C
    H = RET_HEADS
    dm, cd, sd, cdec = _ret_tables(decay_logit, C)

    def row(d, b, c):
        return b * NC + c + d * (NC - 1 - 2 * c)

    kcol0 = RET_QK_W // RET_DK
    vcol0 = 2 * RET_QK_W // RET_DV
    return pl.pallas_call(
        _ret_kernel,
        grid=(2, batch, H, NC),
        in_specs=[
            pl.BlockSpec(memory_space=pltpu.SMEM),
            pl.BlockSpec((C, RET_DK), lambda d, b, h, c: (row(d, b, c), h)),
            pl.BlockSpec((C, RET_DK), lambda d, b, h, c: (row(d, b, c), kcol0 + h)),
            pl.BlockSpec((C, RET_DV), lambda d, b, h, c: (row(d, b, c), vcol0 + h)),
            pl.BlockSpec((None, None, C, C), lambda d, b, h, c: (d, h, 0, 0)),
            pl.BlockSpec((None, None, C, 1), lambda d, b, h, c: (d, h, 0, 0)),
            pl.BlockSpec((None, None, C, 1), lambda d, b, h, c: (d, h, 0, 0)),
        ],
        out_specs=pl.BlockSpec((None, C, RET_DV), lambda d, b, h, c: (d, row(d, b, c), h)),
        out_shape=jax.ShapeDtypeStruct((2, T, RET_V_W), F32),
        scratch_shapes=[pltpu.VMEM((RET_DK, RET_DV), F32)],
        compiler_params=_cparams(("parallel", "parallel", "parallel", "arbitrary")),
        name="retention",
    )(cdec, proj, proj, proj, dm, cd, sd)


def _layer_norm(u, g, b):
    mu = jnp.mean(u, axis=-1, keepdims=True)
    var = jnp.mean(jnp.square(u - mu), axis=-1, keepdims=True)
    return (u - mu) * lax.rsqrt(var + LN_EPS) * g + b


def _mix_out_kernel(*refs, ret):
    if ret:
        y_ref, g_ref, x_ref, w_ref, lng_ref, lnb_ref, wr_ref, o_ref, aff_ref, wbf_ref = refs
    else:
        y_ref, x_ref, w_ref, lng_ref, lnb_ref, wr_ref, o_ref, aff_ref, wbf_ref = refs

    @pl.when(pl.program_id(0) == 0)
    def _():
        wbf_ref[...] = w_ref[...].astype(BF16)

    if ret:
        y = y_ref[0] + y_ref[1]
        parts = []
        for h in range(RET_HEADS):
            yh = y[:, h * RET_DV:(h + 1) * RET_DV]
            mu = jnp.mean(yh, axis=-1, keepdims=True)
            var = jnp.mean(jnp.square(yh - mu), axis=-1, keepdims=True)
            yn = (yh - mu) * lax.rsqrt(var + GN_EPS)
            gh = g_ref[:, h * RET_DV:(h + 1) * RET_DV]
            parts.append((gh * jax.nn.sigmoid(gh) * yn).astype(BF16))
        z = jnp.concatenate(parts, axis=1)
    else:
        z = y_ref[...].astype(BF16)
    mixed = jnp.dot(z, wbf_ref[...], preferred_element_type=F32)
    xn = _layer_norm(DN_ALPHA * x_ref[...] + mixed, lng_ref[...], lnb_ref[...])
    o_ref[...] = xn
    logits = lax.dot_general(wr_ref[...].astype(BF16), xn.astype(BF16), (((1,), (1,)), ((), ())),
                             preferred_element_type=F32)
    e = jnp.exp(logits - jnp.max(logits, axis=0, keepdims=True))
    aff_ref[...] = e / jnp.sum(e, axis=0, keepdims=True)


def _mix_out(y, gsrc, x2d, w_out, ln_g, ln_b, w_router, *, ret):
    T = x2d.shape[0]
    K = w_out.shape[0]
    tm = 256
    wr_t = w_router.T
    row = lambda i: (i, 0)
    const = lambda i: (0, 0)
    if ret:
        gcol = (2 * RET_QK_W + RET_V_W) // RET_V_W
        in_specs = [pl.BlockSpec((2, tm, K), lambda i: (0, i, 0)),
                    pl.BlockSpec((tm, K), lambda i: (i, gcol))]
        args = [y, gsrc]
    else:
        in_specs = [pl.BlockSpec((tm, K), row)]
        args = [y]
    in_specs += [pl.BlockSpec((tm, D_MODEL), row),
                 pl.BlockSpec((K, D_MODEL), const),
                 pl.BlockSpec((1, D_MODEL), const),
                 pl.BlockSpec((1, D_MODEL), const),
                 pl.BlockSpec((N_EXPERTS, D_MODEL), const)]
    args += [x2d, w_out, ln_g.reshape(1, -1), ln_b.reshape(1, -1), wr_t]
    return pl.pallas_call(
        functools.partial(_mix_out_kernel, ret=ret),
        grid=(T // tm,),
        in_specs=in_specs,
        out_specs=[pl.BlockSpec((tm, D_MODEL), row),
                   pl.BlockSpec((N_EXPERTS, tm), lambda i: (0, i))],
        out_shape=[jax.ShapeDtypeStruct((T, D_MODEL), F32),
                   jax.ShapeDtypeStruct((N_EXPERTS, T), F32)],
        scratch_shapes=[pltpu.VMEM((K, D_MODEL), BF16)],
        compiler_params=_cparams(("arbitrary",)),
        name="mix_out_ret" if ret else "mix_out_na",
    )(*args)


def _na_kernel(q_ref, k0_ref, k1_ref, k2_ref, v0_ref, v1_ref, v2_ref, bias_ref, o_ref,
               ka_ref, kb_ref, va_ref, vb_ref, *, rows):
    i = pl.program_id(2)
    R = NA_ROWS_PER_STEP
    lo, hi = NA_WIN_STARTS[1], NA_WIN_STARTS[2] + NA_WIN_COLS
    for t, (kr, vr) in enumerate(((k0_ref, v0_ref), (k1_ref, v1_ref), (k2_ref, v2_ref))):
        ka_ref[t * R:(t + 1) * R] = kr[...].astype(BF16)
        kb_ref[t * R:(t + 1) * R] = kr[:, lo:hi, :].astype(BF16)
        va_ref[t * R:(t + 1) * R] = vr[...].astype(BF16)
        vb_ref[t * R:(t + 1) * R] = vr[:, lo:hi, :].astype(BF16)

    nrow = NA_HEAD_GROUP * NA_QGROUP
    width = NA_HEAD_GROUP * NA_HD
    rid = lax.broadcasted_iota(jnp.int32, (nrow, width), 0) // NA_QGROUP
    cid = lax.broadcasted_iota(jnp.int32, (nrow, width), 1) // NA_HD
    head_mask = (rid == cid).astype(F32)
    nkeys = NA_KH * NA_WIN_COLS

    def body(rr, carry):
        r = i * R + rr
        rs = jnp.clip(r - NA_KH // 2, 0, rows - NA_KH)
        delta = r - rs
        rl = rs - (i * R - R)
        for j, c0 in enumerate(NA_WIN_STARTS):
            if j in (0, 3):
                kw = ka_ref[pl.ds(rl, NA_KH), c0:c0 + NA_WIN_COLS, :]
                vw = va_ref[pl.ds(rl, NA_KH), c0:c0 + NA_WIN_COLS, :]
            else:
                kw = kb_ref[pl.ds(rl, NA_KH), c0 - lo:c0 - lo + NA_WIN_COLS, :]
                vw = vb_ref[pl.ds(rl, NA_KH), c0 - lo:c0 - lo + NA_WIN_COLS, :]
            kw = kw.reshape(nkeys, width)
            vw = vw.reshape(nkeys, width)
            qg = q_ref[rr, j * NA_QGROUP:(j + 1) * NA_QGROUP, :]
            lhs = (jnp.tile(qg, (NA_HEAD_GROUP, 1)) * head_mask).astype(BF16)
            s = lax.dot_general(lhs, kw, (((1,), (1,)), ((), ())), preferred_element_type=F32)
            s = s + bias_ref[delta, j]
            e = jnp.exp(s - jnp.max(s, axis=-1, keepdims=True))
            p = e / jnp.sum(e, axis=-1, keepdims=True)
            pv = jnp.dot(p.astype(BF16), vw, preferred_element_type=F32) * head_mask
            out = pv[0:NA_QGROUP]
            for g in range(1, NA_HEAD_GROUP):
                out = out + pv[g * NA_QGROUP:(g + 1) * NA_QGROUP]
            o_ref[rr, j * NA_QGROUP:(j + 1) * NA_QGROUP, :] = out
        return carry

    lax.fori_loop(0, R, body, 0)


def _na_bias_table(rpb):
    HG = NA_HEADS // NA_HEAD_GROUP
    nj = GRID_W // NA_QGROUP
    delta = jnp.arange(NA_KH)[:, None, None, None, None]
    j = jnp.arange(nj)[None, :, None, None, None]
    q = jnp.arange(NA_QGROUP)[None, None, :, None, None]
    a = jnp.arange(NA_KH)[None, None, None, :, None]
    cc = jnp.arange(NA_WIN_COLS)[None, None, None, None, :]
    w = j * NA_QGROUP + q
    col = jnp.asarray(NA_WIN_STARTS)[j] + cc
    cs = jnp.clip(w - NA_KW // 2, 0, GRID_W - NA_KW)
    valid = (col >= cs) & (col < cs + NA_KW)
    row_off = jnp.broadcast_to(a - delta + (NA_KH - 1), (NA_KH, nj, NA_QGROUP, NA_KH, NA_WIN_COLS))
    col_off = jnp.broadcast_to(jnp.clip(col - w + (NA_KW - 1), 0, 2 * NA_KW - 2), row_off.shape)
    vals = rpb[:, row_off, col_off]
    vals = jnp.where(valid[None], vals, NEG_BIG)
    vals = vals.reshape(HG, NA_HEAD_GROUP, NA_KH, nj, NA_QGROUP, NA_KH * NA_WIN_COLS)
    vals = vals.transpose(0, 2, 3, 1, 4, 5)
    return vals.reshape(HG, NA_KH, nj, NA_HEAD_GROUP * NA_QGROUP, NA_KH * NA_WIN_COLS).astype(F32)


def _na_core(qkv, rpb, *, batch, seq):
    T = qkv.shape[0]
    rows = seq // GRID_W
    R = NA_ROWS_PER_STEP
    assert rows % R == 0 and rows >= NA_KH
    nb = rows // R
    HG = NA_HEADS // NA_HEAD_GROUP
    width = NA_HEAD_GROUP * NA_HD
    qkv3 = qkv.reshape(T // GRID_W, GRID_W, 3 * D_MODEL)
    bias = _na_bias_table(rpb)
    kcol0 = D_MODEL // width
    vcol0 = 2 * D_MODEL // width
    blk = (R, GRID_W, width)

    def kv_spec(col0, off):
        return pl.BlockSpec(blk, lambda g, b, i: (b * nb + jnp.clip(i + off, 0, nb - 1), 0, col0 + g))

    span = NA_WIN_STARTS[2] + NA_WIN_COLS - NA_WIN_STARTS[1]
    out = pl.pallas_call(
        functools.partial(_na_kernel, rows=rows),
        grid=(HG, batch, nb),
        in_specs=[pl.BlockSpec(blk, lambda g, b, i: (b * nb + i, 0, g)),
                  kv_spec(kcol0, -1), kv_spec(kcol0, 0), kv_spec(kcol0, 1),
                  kv_spec(vcol0, -1), kv_spec(vcol0, 0), kv_spec(vcol0, 1),
                  pl.BlockSpec((None,) + bias.shape[1:], lambda g, b, i: (g, 0, 0, 0, 0))],
        out_specs=pl.BlockSpec(blk, lambda g, b, i: (b * nb + i, 0, g)),
        out_shape=jax.ShapeDtypeStruct((T // GRID_W, GRID_W, D_MODEL), F32),
        scratch_shapes=[pltpu.VMEM((3 * R, GRID_W, width), BF16),
                        pltpu.VMEM((3 * R, span, width), BF16),
                        pltpu.VMEM((3 * R, GRID_W, width), BF16),
                        pltpu.VMEM((3 * R, span, width), BF16)],
        compiler_params=_cparams(("parallel", "parallel", "parallel")),
        name="natten",
    )(qkv3, qkv3, qkv3, qkv3, qkv3, qkv3, qkv3, bias)
    return out.reshape(T, D_MODEL)


def _ffn_kernel(x_ref, gate_ref, wg_ref, wu_ref, wd_ref, o_ref, wgb_ref, wub_ref, wdb_ref):
    @pl.when(pl.program_id(1) == 0)
    def _():
        wgb_ref[...] = wg_ref[...].astype(BF16)
        wub_ref[...] = wu_ref[...].astype(BF16)
        wdb_ref[...] = wd_ref[...].astype(BF16)

    x = x_ref[...].astype(BF16)
    a = jnp.dot(x, wgb_ref[...], preferred_element_type=F32)
    b = jnp.dot(x, wub_ref[...], preferred_element_type=F32)
    hidden = (a * jax.nn.sigmoid(a) * b).astype(BF16)
    o_ref[...] = jnp.dot(hidden, wdb_ref[...], preferred_element_type=F32) * gate_ref[...]


def _expert_ffn(xe, gates, w_gate, w_up, w_down):
    E, cap, D = xe.shape
    Fh = w_gate.shape[-1]
    tm = 256
    wspec = lambda a, b: pl.BlockSpec((None, a, b), lambda e, i: (e, 0, 0))
    return pl.pallas_call(
        _ffn_kernel,
        grid=(E, cap // tm),
        in_specs=[pl.BlockSpec((None, tm, D), lambda e, i: (e, i, 0)),
                  pl.BlockSpec((None, tm, 1), lambda e, i: (e, i, 0)),
                  wspec(D, Fh), wspec(D, Fh), wspec(Fh, D)],
        out_specs=pl.BlockSpec((None, tm, D), lambda e, i: (e, i, 0)),
        out_shape=jax.ShapeDtypeStruct((E, cap, D), F32),
        scratch_shapes=[pltpu.VMEM((D, Fh), BF16), pltpu.VMEM((D, Fh), BF16), pltpu.VMEM((Fh, D), BF16)],
        compiler_params=_cparams(("parallel", "arbitrary")),
        name="expert_ffn",
    )(xe, gates[..., None], w_gate, w_up, w_down)


def _add_ln_kernel(x_ref, f_ref, g_ref, b_ref, o_ref):
    o_ref[...] = _layer_norm(DN_ALPHA * x_ref[...] + f_ref[...], g_ref[...], b_ref[...])


def _add_ln(x2d, f2d, ln_g, ln_b):
    T, D = x2d.shape
    tm = 512
    row = lambda i: (i, 0)
    const = lambda i: (0, 0)
    return pl.pallas_call(
        _add_ln_kernel,
        grid=(T // tm,),
        in_specs=[pl.BlockSpec((tm, D), row), pl.BlockSpec((tm, D), row),
                  pl.BlockSpec((1, D), const), pl.BlockSpec((1, D), const)],
        out_specs=pl.BlockSpec((tm, D), row),
        out_shape=jax.ShapeDtypeStruct((T, D), F32),
        compiler_params=_cparams(("parallel",)),
        name="add_ln",
    )(x2d, f2d, ln_g.reshape(1, -1), ln_b.reshape(1, -1))


def _moe(x2d, aff_t, w_gate, w_up, w_down):
    T, D = x2d.shape
    cap = CAPACITY_FACTOR * T // N_EXPERTS
    gates, idx = lax.top_k(aff_t, cap)
    xe = jnp.take(x2d, idx, axis=0)
    ye = _expert_ffn(xe, gates, w_gate, w_up, w_down)
    return jnp.zeros((T, D), F32).at[idx.reshape(-1)].add(ye.reshape(-1, D))


def _trunk(x, ret_w_in, ret_decay, ret_w_out, na_w_in, na_rpb, na_w_out,
           ln_mix_g, ln_mix_b, ln_ffn_g, ln_ffn_b, w_router, w_gate, w_up, w_down):
    B, S, D = x.shape
    x2d = x.reshape(B * S, D)
    cos, sin = _rotary_tables(S)
    for i in range(DEPTH):
        j = i // 2
        if i % 2 == 0:
            proj = _proj(x2d, ret_w_in[j], cos, sin, mode="ret", seq=S)
            y = _retention_core(proj, ret_decay[j], batch=B, seq=S)
            x2d, aff_t = _mix_out(y, proj, x2d, ret_w_out[j], ln_mix_g[i], ln_mix_b[i], w_router[i], ret=True)
        else:
            qkv = _proj(x2d, na_w_in[j], cos, sin, mode="na", seq=S)
            o = _na_core(qkv, na_rpb[j], batch=B, seq=S)
            x2d, aff_t = _mix_out(o, None, x2d, na_w_out[j], ln_mix_g[i], ln_mix_b[i], w_router[i], ret=False)
        f = _moe(x2d, aff_t, w_gate[i], w_up[i], w_down[i])
        x2d = _add_ln(x2d, f, ln_ffn_g[i], ln_ffn_b[i])
    return x2d.reshape(B, S, D)


def kernel(x_prompt, x_sample, ret_w_in, ret_decay, ret_w_out, na_w_in, na_rpb, na_w_out, ln_mix_g, ln_mix_b,
           ln_ffn_g, ln_ffn_b, w_router, w_gate, w_up, w_down):
    weights = (ret_w_in, ret_decay, ret_w_out, na_w_in, na_rpb, na_w_out, ln_mix_g, ln_mix_b,
               ln_ffn_g, ln_ffn_b, w_router, w_gate, w_up, w_down)
    return _trunk(x_prompt, *weights), _trunk(x_sample, *weights)
```

```python
import functools

import jax
import jax.numpy as jnp
import numpy as np
from jax import lax
from jax.experimental import pallas as pl
from jax.experimental.pallas import tpu as pltpu

F32 = jnp.float32
BF16 = jnp.bfloat16

D_MODEL = 1024
DEPTH = 2
GRID_W = 64
RET_HEADS = 4
RET_DK = 256
RET_DV = 512
RET_QK_W = RET_HEADS * RET_DK
RET_V_W = RET_HEADS * RET_DV
ROPE_BASE = 10000.0
NA_HEADS = 32
NA_HD = D_MODEL // NA_HEADS
NA_KH = 8
NA_KW = 16
N_EXPERTS = 16
EXPERT_FF = 1024
CAPACITY_FACTOR = 2
DN_ALPHA = (2 * DEPTH) ** 0.25
LN_EPS = 1e-5
GN_EPS = 1e-6

VMEM_LIMIT_BYTES = 56 * 1024 * 1024
PROJ_COL_CHUNK = 1024
RET_CHUNK = 256
NA_ROWS_PER_STEP = 8
NA_HEAD_GROUP = 8
NA_QGROUP = 16
NA_WIN_COLS = 32
NA_WIN_STARTS = (0, 8, 24, 32)
NEG_BIG = -1e30
COMB_TOKENS = 256
COMB_WIN = 64
COMB_FAST_MAX = COMB_WIN - 8
COMB_SLOW_WIN = COMB_TOKENS + 8
COMB_SLOW_PAD = 384


def _cparams(sem):
    return pltpu.CompilerParams(dimension_semantics=sem, vmem_limit_bytes=VMEM_LIMIT_BYTES)


def _proj_kernel(x_ref, w_ref, cos_ref, sin_ref, o_ref, *, mode):
    xb = x_ref[...].astype(BF16)
    tn = PROJ_COL_CHUNK
    for c in range(w_ref.shape[1] // tn):
        acc = jnp.dot(xb, w_ref[:, c * tn:(c + 1) * tn], preferred_element_type=F32)
        if mode == "na" and c == 0:
            o_ref[:, 0:tn] = acc * (NA_HD ** -0.5)
        elif mode == "ret" and c < 2:
            scale = RET_DK ** -0.5 if c == 1 else 1.0
            cs = cos_ref[...]
            sn = sin_ref[...]
            half = RET_DK // 2
            for h in range(RET_HEADS):
                lo = c * tn + h * RET_DK
                x1 = acc[:, h * RET_DK:h * RET_DK + half] * scale
                x2 = acc[:, h * RET_DK + half:(h + 1) * RET_DK] * scale
                o_ref[:, lo:lo + half] = x1 * cs - x2 * sn
                o_ref[:, lo + half:lo + RET_DK] = x1 * sn + x2 * cs
        else:
            o_ref[:, c * tn:(c + 1) * tn] = acc


def _proj(x2d, w, cos, sin, *, mode, seq):
    T, K = x2d.shape
    N = w.shape[1]
    tm = 256
    nseq = seq // tm
    return pl.pallas_call(
        functools.partial(_proj_kernel, mode=mode),
        grid=(T // tm,),
        in_specs=[
            pl.BlockSpec((tm, K), lambda i: (i, 0)),
            pl.BlockSpec((K, N), lambda i: (0, 0)),
            pl.BlockSpec((tm, RET_DK // 2), lambda i: (i % nseq, 0)),
            pl.BlockSpec((tm, RET_DK // 2), lambda i: (i % nseq, 0)),
        ],
        out_specs=pl.BlockSpec((tm, N), lambda i: (i, 0)),
        out_shape=jax.ShapeDtypeStruct((T, N), F32),
        compiler_params=_cparams(("parallel",)),
        name="proj_" + mode,
    )(x2d, w.astype(BF16), cos, sin)


def _rotary_tables(seq):
    inv = 1.0 / (ROPE_BASE ** (jnp.arange(0, RET_DK, 2, dtype=F32) / RET_DK))
    ang = jnp.arange(seq, dtype=F32)[:, None] * inv[None, :]
    return jnp.cos(ang), jnp.sin(ang)


def _ret_kernel(cdec_ref, q_ref, k_ref, v_ref, dm_ref, cd_ref, sd_ref, o_ref, state_ref):
    d = pl.program_id(0)
    c = pl.program_id(2)

    @pl.when(c == 0)
    def _():
        state_ref[...] = jnp.zeros_like(state_ref)

    for h in range(RET_HEADS):
        q = q_ref[:, h * RET_DK:(h + 1) * RET_DK].astype(BF16)
        k = k_ref[:, h * RET_DK:(h + 1) * RET_DK]
        v = v_ref[:, h * RET_DV:(h + 1) * RET_DV].astype(BF16)
        scores = lax.dot_general(q, k.astype(BF16), (((1,), (1,)), ((), ())),
                                 preferred_element_type=F32) * dm_ref[h]
        o = jnp.dot(scores.astype(BF16), v, preferred_element_type=F32)
        state = state_ref[h]
        o = o + jnp.dot(q, state.astype(BF16), preferred_element_type=F32) * cd_ref[h]
        ks = (k * sd_ref[h]).astype(BF16)
        state_ref[h] = state * cdec_ref[d * RET_HEADS + h] + lax.dot_general(
            ks, v, (((0,), (0,)), ((), ())), preferred_element_type=F32)
        o_ref[:, h * RET_DV:(h + 1) * RET_DV] = o


def _ret_tables(decay_logit, C):
    log_g = jax.nn.log_sigmoid(decay_logit.astype(F32))
    i = jnp.arange(C, dtype=F32)
    diff = i[:, None] - i[None, :]
    lf = log_g[0][:, None, None]
    lb = log_g[1][:, None, None]
    dm_f = jnp.where((diff >= 0)[None], jnp.exp(lf * jnp.maximum(diff, 0.0)[None]), 0.0)
    dm_b = jnp.where((diff < 0)[None], jnp.exp(lb * jnp.maximum(-diff, 0.0)[None]), 0.0)
    cd_f = jnp.exp(log_g[0][:, None] * (i[None, :] + 1.0))
    cd_b = jnp.exp(log_g[1][:, None] * (C - i[None, :]))
    sd_f = jnp.exp(log_g[0][:, None] * (C - 1.0 - i[None, :]))
    sd_b = jnp.exp(log_g[1][:, None] * i[None, :])
    dm = jnp.stack([dm_f, dm_b]).astype(F32)
    cd = jnp.stack([cd_f, cd_b]).astype(F32)[..., None]
    sd = jnp.stack([sd_f, sd_b]).astype(F32)[..., None]
    cdec = jnp.exp(log_g * C).reshape(-1).astype(F32)
    return dm, cd, sd, cdec


def _retention_core(proj, decay_logit, *, batch, seq):
    T = proj.shape[0]
    C = RET_CHUNK
    NC = seq // C
    H = RET_HEADS
    dm, cd, sd, cdec = _ret_tables(decay_logit, C)

    def row(d, b, c):
        return b * NC + c + d * (NC - 1 - 2 * c)

    assert 2 * RET_QK_W == RET_V_W
    return pl.pallas_call(
        _ret_kernel,
        grid=(2, batch, NC),
        in_specs=[
            pl.BlockSpec(memory_space=pltpu.SMEM),
            pl.BlockSpec((C, RET_QK_W), lambda d, b, c: (row(d, b, c), 0)),
            pl.BlockSpec((C, RET_QK_W), lambda d, b, c: (row(d, b, c), 1)),
            pl.BlockSpec((C, RET_V_W), lambda d, b, c: (row(d, b, c), 1)),
            pl.BlockSpec((None, H, C, C), lambda d, b, c: (d, 0, 0, 0)),
            pl.BlockSpec((None, H, C, 1), lambda d, b, c: (d, 0, 0, 0)),
            pl.BlockSpec((None, H, C, 1), lambda d, b, c: (d, 0, 0, 0)),
        ],
        out_specs=pl.BlockSpec((None, C, RET_V_W), lambda d, b, c: (d, row(d, b, c), 0)),
        out_shape=jax.ShapeDtypeStruct((2, T, RET_V_W), F32),
        scratch_shapes=[pltpu.VMEM((H, RET_DK, RET_DV), F32)],
        compiler_params=_cparams(("parallel", "parallel", "arbitrary")),
        name="retention",
    )(cdec, proj, proj, proj, dm, cd, sd)


def _layer_norm(u, g, b):
    mu = jnp.mean(u, axis=-1, keepdims=True)
    var = jnp.mean(jnp.square(u - mu), axis=-1, keepdims=True)
    return (u - mu) * lax.rsqrt(var + LN_EPS) * g + b


def _mix_out_kernel(*refs, ret):
    if ret:
        y_ref, g_ref, x_ref, w_ref, lng_ref, lnb_ref, wr_ref, o_ref, aff_ref, wbf_ref = refs
    else:
        y_ref, x_ref, w_ref, lng_ref, lnb_ref, wr_ref, o_ref, aff_ref, wbf_ref = refs

    @pl.when(pl.program_id(0) == 0)
    def _():
        wbf_ref[...] = w_ref[...].astype(BF16)

    if ret:
        y = y_ref[0] + y_ref[1]
        parts = []
        for h in range(RET_HEADS):
            yh = y[:, h * RET_DV:(h + 1) * RET_DV]
            mu = jnp.mean(yh, axis=-1, keepdims=True)
            var = jnp.mean(jnp.square(yh - mu), axis=-1, keepdims=True)
            yn = (yh - mu) * lax.rsqrt(var + GN_EPS)
            gh = g_ref[:, h * RET_DV:(h + 1) * RET_DV]
            parts.append((gh * jax.nn.sigmoid(gh) * yn).astype(BF16))
        z = jnp.concatenate(parts, axis=1)
    else:
        z = y_ref[...].astype(BF16)
    mixed = jnp.dot(z, wbf_ref[...], preferred_element_type=F32)
    xn = _layer_norm(DN_ALPHA * x_ref[...] + mixed, lng_ref[...], lnb_ref[...])
    o_ref[...] = xn
    logits = lax.dot_general(wr_ref[...].astype(BF16), xn.astype(BF16), (((1,), (1,)), ((), ())),
                             preferred_element_type=F32)
    e = jnp.exp(logits - jnp.max(logits, axis=0, keepdims=True))
    aff_ref[...] = e / jnp.sum(e, axis=0, keepdims=True)


def _mix_out(y, gsrc, x2d, w_out, ln_g, ln_b, w_router, *, ret):
    T = x2d.shape[0]
    K = w_out.shape[0]
    tm = 256
    wr_t = w_router.T
    row = lambda i: (i, 0)
    const = lambda i: (0, 0)
    if ret:
        gcol = (2 * RET_QK_W + RET_V_W) // RET_V_W
        in_specs = [pl.BlockSpec((2, tm, K), lambda i: (0, i, 0)),
                    pl.BlockSpec((tm, K), lambda i: (i, gcol))]
        args = [y, gsrc]
    else:
        in_specs = [pl.BlockSpec((tm, K), row)]
        args = [y]
    in_specs += [pl.BlockSpec((tm, D_MODEL), row),
                 pl.BlockSpec((K, D_MODEL), const),
                 pl.BlockSpec((1, D_MODEL), const),
                 pl.BlockSpec((1, D_MODEL), const),
                 pl.BlockSpec((N_EXPERTS, D_MODEL), const)]
    args += [x2d, w_out, ln_g.reshape(1, -1), ln_b.reshape(1, -1), wr_t]
    return pl.pallas_call(
        functools.partial(_mix_out_kernel, ret=ret),
        grid=(T // tm,),
        in_specs=in_specs,
        out_specs=[pl.BlockSpec((tm, D_MODEL), row),
                   pl.BlockSpec((N_EXPERTS, tm), lambda i: (0, i))],
        out_shape=[jax.ShapeDtypeStruct((T, D_MODEL), F32),
                   jax.ShapeDtypeStruct((N_EXPERTS, T), F32)],
        scratch_shapes=[pltpu.VMEM((K, D_MODEL), BF16)],
        compiler_params=_cparams(("arbitrary",)),
        name="mix_out_ret" if ret else "mix_out_na",
    )(*args)


def _na_kernel(q_ref, k0_ref, k1_ref, k2_ref, v0_ref, v1_ref, v2_ref, bias_ref, o_ref,
               ka_ref, kb_ref, va_ref, vb_ref, *, rows):
    i = pl.program_id(2)
    R = NA_ROWS_PER_STEP
    lo, hi = NA_WIN_STARTS[1], NA_WIN_STARTS[2] + NA_WIN_COLS
    for t, (kr, vr) in enumerate(((k0_ref, v0_ref), (k1_ref, v1_ref), (k2_ref, v2_ref))):
        ka_ref[t * R:(t + 1) * R] = kr[...].astype(BF16)
        kb_ref[t * R:(t + 1) * R] = kr[:, lo:hi, :].astype(BF16)
        va_ref[t * R:(t + 1) * R] = vr[...].astype(BF16)
        vb_ref[t * R:(t + 1) * R] = vr[:, lo:hi, :].astype(BF16)

    nrow = NA_HEAD_GROUP * NA_QGROUP
    width = NA_HEAD_GROUP * NA_HD
    rid = lax.broadcasted_iota(jnp.int32, (nrow, width), 0) // NA_QGROUP
    cid = lax.broadcasted_iota(jnp.int32, (nrow, width), 1) // NA_HD
    head_mask = (rid == cid).astype(F32)
    nkeys = NA_KH * NA_WIN_COLS

    def body(rr, carry):
        r = i * R + rr
        rs = jnp.clip(r - NA_KH // 2, 0, rows - NA_KH)
        delta = r - rs
        rl = rs - (i * R - R)
        def window(a_ref, b_ref, j):
            c0 = NA_WIN_STARTS[j]
            if j in (0, 3):
                win = a_ref[pl.ds(rl, NA_KH), c0:c0 + NA_WIN_COLS, :]
            else:
                win = b_ref[pl.ds(rl, NA_KH), c0 - lo:c0 - lo + NA_WIN_COLS, :]
            return win.reshape(nkeys, width)

        scores = []
        for j in range(len(NA_WIN_STARTS)):
            qg = q_ref[rr, j * NA_QGROUP:(j + 1) * NA_QGROUP, :]
            lhs = (jnp.tile(qg, (NA_HEAD_GROUP, 1)) * head_mask).astype(BF16)
            scores.append(lax.dot_general(lhs, window(ka_ref, kb_ref, j), (((1,), (1,)), ((), ())),
                                          preferred_element_type=F32))
        s = jnp.concatenate(scores, axis=0) + bias_ref[delta].reshape(len(scores) * nrow, nkeys)
        e = jnp.exp(s - jnp.max(s, axis=-1, keepdims=True))
        p = (e * (1.0 / jnp.sum(e, axis=-1, keepdims=True))).astype(BF16)
        for j in range(len(NA_WIN_STARTS)):
            pv = jnp.dot(p[j * nrow:(j + 1) * nrow], window(va_ref, vb_ref, j),
                         preferred_element_type=F32) * head_mask
            out = pv[0:NA_QGROUP]
            for g in range(1, NA_HEAD_GROUP):
                out = out + pv[g * NA_QGROUP:(g + 1) * NA_QGROUP]
            o_ref[rr, j * NA_QGROUP:(j + 1) * NA_QGROUP, :] = out
        return carry

    lax.fori_loop(0, R, body, 0, unroll=True)


def _na_bias_table(rpb):
    HG = NA_HEADS // NA_HEAD_GROUP
    nj = GRID_W // NA_QGROUP
    delta = np.arange(NA_KH)[:, None]
    a = np.arange(NA_KH)[None, :]
    row_sel = np.eye(2 * NA_KH - 1, dtype=np.float32)[a - delta + (NA_KH - 1)]
    w = (np.arange(nj)[:, None] * NA_QGROUP + np.arange(NA_QGROUP)[None, :])[:, :, None]
    col = np.asarray(NA_WIN_STARTS)[:, None, None] + np.arange(NA_WIN_COLS)[None, None, :]
    cs = np.clip(w - NA_KW // 2, 0, GRID_W - NA_KW)
    valid = (col >= cs) & (col < cs + NA_KW)
    col_off = np.clip(col - w + (NA_KW - 1), 0, 2 * NA_KW - 2)
    col_sel = np.eye(2 * NA_KW - 1, dtype=np.float32)[col_off] * valid[..., None]
    rpb_g = rpb.astype(F32).reshape(HG, NA_HEAD_GROUP, 2 * NA_KH - 1, 2 * NA_KW - 1)
    vals = jnp.einsum("ghrc,dar,jqkc->gdjhqak", rpb_g, row_sel, col_sel, precision=lax.Precision.HIGHEST)
    vals = jnp.where(valid[None, None, :, None, :, None, :], vals, NEG_BIG)
    return vals.reshape(HG, NA_KH, nj, NA_HEAD_GROUP * NA_QGROUP, NA_KH * NA_WIN_COLS)


def _na_core(qkv, rpb, *, batch, seq):
    T = qkv.shape[0]
    rows = seq // GRID_W
    R = NA_ROWS_PER_STEP
    assert rows % R == 0 and rows >= NA_KH
    nb = rows // R
    HG = NA_HEADS // NA_HEAD_GROUP
    width = NA_HEAD_GROUP * NA_HD
    qkv3 = qkv.reshape(T // GRID_W, GRID_W, 3 * D_MODEL)
    bias = _na_bias_table(rpb)
    kcol0 = D_MODEL // width
    vcol0 = 2 * D_MODEL // width
    blk = (R, GRID_W, width)

    def kv_spec(col0, off):
        return pl.BlockSpec(blk, lambda g, b, i: (b * nb + jnp.clip(i + off, 0, nb - 1), 0, col0 + g))

    span = NA_WIN_STARTS[2] + NA_WIN_COLS - NA_WIN_STARTS[1]
    out = pl.pallas_call(
        functools.partial(_na_kernel, rows=rows),
        grid=(HG, batch, nb),
        in_specs=[pl.BlockSpec(blk, lambda g, b, i: (b * nb + i, 0, g)),
                  kv_spec(kcol0, -1), kv_spec(kcol0, 0), kv_spec(kcol0, 1),
                  kv_spec(vcol0, -1), kv_spec(vcol0, 0), kv_spec(vcol0, 1),
                  pl.BlockSpec((None,) + bias.shape[1:], lambda g, b, i: (g, 0, 0, 0, 0))],
        out_specs=pl.BlockSpec(blk, lambda g, b, i: (b * nb + i, 0, g)),
        out_shape=jax.ShapeDtypeStruct((T // GRID_W, GRID_W, D_MODEL), F32),
        scratch_shapes=[pltpu.VMEM((3 * R, GRID_W, width), BF16),
                        pltpu.VMEM((3 * R, span, width), BF16),
                        pltpu.VMEM((3 * R, GRID_W, width), BF16),
                        pltpu.VMEM((3 * R, span, width), BF16)],
        compiler_params=_cparams(("parallel", "parallel", "parallel")),
        name="natten",
    )(qkv3, qkv3, qkv3, qkv3, qkv3, qkv3, qkv3, bias)
    return out.reshape(T, D_MODEL)


def _ffn_kernel(x_ref, gate_ref, wg_ref, wu_ref, wd_ref, o_ref, wgb_ref, wub_ref, wdb_ref):
    @pl.when(pl.program_id(1) == 0)
    def _():
        wgb_ref[...] = wg_ref[...].astype(BF16)
        wub_ref[...] = wu_ref[...].astype(BF16)
        wdb_ref[...] = wd_ref[...].astype(BF16)

    x = x_ref[...].astype(BF16)
    a = jnp.dot(x, wgb_ref[...], preferred_element_type=F32)
    b = jnp.dot(x, wub_ref[...], preferred_element_type=F32)
    hidden = (a * jax.nn.sigmoid(a) * b).astype(BF16)
    o_ref[...] = jnp.dot(hidden, wdb_ref[...], preferred_element_type=F32) * gate_ref[...]


def _expert_ffn(xe, gates, w_gate, w_up, w_down):
    E, cap, D = xe.shape
    Fh = w_gate.shape[-1]
    tm = 256
    wspec = lambda a, b: pl.BlockSpec((None, a, b), lambda e, i: (e, 0, 0))
    return pl.pallas_call(
        _ffn_kernel,
        grid=(E, cap // tm),
        in_specs=[pl.BlockSpec((None, tm, D), lambda e, i: (e, i, 0)),
                  pl.BlockSpec((None, tm, 1), lambda e, i: (e, i, 0)),
                  wspec(D, Fh), wspec(D, Fh), wspec(Fh, D)],
        out_specs=pl.BlockSpec((None, tm, D), lambda e, i: (e, i, 0)),
        out_shape=jax.ShapeDtypeStruct((E, cap, D), F32),
        scratch_shapes=[pltpu.VMEM((D, Fh), BF16), pltpu.VMEM((D, Fh), BF16), pltpu.VMEM((Fh, D), BF16)],
        compiler_params=_cparams(("parallel", "arbitrary")),
        name="expert_ffn",
    )(xe, gates[..., None], w_gate, w_up, w_down)


def _split_dot(p, rows):
    hi = rows.astype(BF16)
    lo = (rows - hi.astype(F32)).astype(BF16)
    return jnp.dot(p, hi, preferred_element_type=F32) + jnp.dot(p, lo, preferred_element_type=F32)


def _combine_kernel(s0_ref, fast_ref, pos_ref, x_ref, g_ref, b_ref, ye_ref, o_ref,
                    stage_ref, slow_ref, f_ref, sem_ref, slow_sem, *, nblk, cap):
    i = pl.program_id(0)
    slot = i % 2
    E = N_EXPERTS
    TB = COMB_TOKENS

    def first_slot(blk, e):
        return s0_ref[e * (nblk + 1) + blk]

    def window_base(blk, e, win):
        return jnp.minimum((first_slot(blk, e) // 8) * 8, cap - win)

    def window_copy(blk, e, sl):
        src = pl.multiple_of(e * cap + window_base(blk, e, COMB_WIN), 8)
        return pltpu.make_async_copy(ye_ref.at[pl.ds(src, COMB_WIN)],
                                     stage_ref.at[sl, pl.ds(e * COMB_WIN, COMB_WIN)], sem_ref.at[sl])

    @pl.when(i == 0)
    def _():
        slow_ref[...] = jnp.zeros_like(slow_ref)
        for e in range(E):
            window_copy(0, e, 0).start()

    @pl.when(i + 1 < nblk)
    def _():
        for e in range(E):
            window_copy(i + 1, e, 1 - slot).start()

    for e in range(E):
        window_copy(i, e, slot).wait()

    pos = pos_ref[...]
    lane_e = lax.broadcasted_iota(jnp.int32, (TB, E), 1)

    @pl.when(fast_ref[i] == 1)
    def _():
        base = jnp.zeros((1, E), jnp.int32)
        for e in range(E):
            base = jnp.where(lane_e[0:1] == e, window_base(i, e, COMB_WIN), base)
        kidx = jnp.where(pos >= 0, pos - base, -1)
        lane = lax.broadcasted_iota(jnp.int32, (TB, 2 * COMB_WIN), 1)
        cols = []
        for c in range(E // 2):
            k0 = kidx[:, 2 * c:2 * c + 1]
            k1 = kidx[:, 2 * c + 1:2 * c + 2]
            k1 = jnp.where(k1 >= 0, k1 + COMB_WIN, -1)
            cols.append(jnp.where((lane == k0) | (lane == k1), 1.0, 0.0).astype(BF16))
        f_ref[...] = _split_dot(jnp.concatenate(cols, axis=1), stage_ref[slot])

    @pl.when(fast_ref[i] == 0)
    def _():
        lane = lax.broadcasted_iota(jnp.int32, (TB, COMB_SLOW_PAD), 1)

        def body(e, acc):
            base = window_base(i, e, COMB_SLOW_WIN)
            src = pl.multiple_of(e * cap + base, 8)
            cp = pltpu.make_async_copy(ye_ref.at[pl.ds(src, COMB_SLOW_WIN)],
                                       slow_ref.at[pl.ds(0, COMB_SLOW_WIN)], slow_sem)
            cp.start()
            cp.wait()
            col = jnp.max(jnp.where(lane_e == e, pos, -1), axis=1, keepdims=True)
            k = jnp.where(col >= 0, col - base, -1)
            p = jnp.where(lane == k, 1.0, 0.0).astype(BF16)
            return acc + _split_dot(p, slow_ref[...])

        f_ref[...] = lax.fori_loop(0, E, body, jnp.zeros(f_ref.shape, F32))

    o_ref[...] = _layer_norm(DN_ALPHA * x_ref[...] + f_ref[...], g_ref[...], b_ref[...])


def _combine(ye2d, pos_t, s0, fast, x2d, ln_g, ln_b, *, cap):
    T, D = x2d.shape
    E = N_EXPERTS
    TB = COMB_TOKENS
    nblk = T // TB
    assert cap >= COMB_SLOW_WIN and cap % 8 == 0 and T % TB == 0
    row = lambda i, s0, fl: (i, 0)
    const = lambda i, s0, fl: (0, 0)
    return pl.pallas_call(
        functools.partial(_combine_kernel, nblk=nblk, cap=cap),
        grid_spec=pltpu.PrefetchScalarGridSpec(
            num_scalar_prefetch=2,
            grid=(nblk,),
            in_specs=[pl.BlockSpec((TB, E), row), pl.BlockSpec((TB, D), row),
                      pl.BlockSpec((1, D), const), pl.BlockSpec((1, D), const),
                      pl.BlockSpec(memory_space=pl.ANY)],
            out_specs=pl.BlockSpec((TB, D), row),
            scratch_shapes=[pltpu.VMEM((2, E * COMB_WIN, D), F32),
                            pltpu.VMEM((COMB_SLOW_PAD, D), F32),
                            pltpu.VMEM((TB, D), F32),
                            pltpu.SemaphoreType.DMA((2,)),
                            pltpu.SemaphoreType.DMA(())]),
        out_shape=jax.ShapeDtypeStruct((T, D), F32),
        compiler_params=_cparams(("arbitrary",)),
        name="combine_ln",
    )(s0, fast, pos_t, x2d, ln_g.reshape(1, -1), ln_b.reshape(1, -1), ye2d)


def _route(aff_t, cap):
    E, T = aff_t.shape
    gates, idx = lax.top_k(aff_t, cap)
    tau = gates[:, cap - 1:cap]
    above = aff_t > tau
    tied = aff_t == tau
    need = cap - jnp.sum(above, axis=1, keepdims=True, dtype=jnp.int32)
    sel = above | (tied & (jnp.cumsum(tied.astype(jnp.int32), axis=1) <= need))
    csum = jnp.cumsum(sel.astype(jnp.int32), axis=1)
    pos_t = jnp.where(sel, csum - 1, -1).T
    s0 = jnp.concatenate([jnp.zeros((E, 1), jnp.int32), csum[:, COMB_TOKENS - 1::COMB_TOKENS]], axis=1)
    fast = jnp.all(s0[:, 1:] - s0[:, :-1] <= COMB_FAST_MAX, axis=0).astype(jnp.int32)
    idx_sorted, gates_sorted = lax.sort((idx, gates), dimension=1, num_keys=1)
    return idx_sorted, gates_sorted, pos_t, s0.reshape(-1), fast


def _moe_ln(x2d, aff_t, w_gate, w_up, w_down, ln_g, ln_b):
    T, D = x2d.shape
    cap = CAPACITY_FACTOR * T // N_EXPERTS
    idx, gates, pos_t, s0, fast = _route(aff_t, cap)
    xe = jnp.take(x2d, idx, axis=0)
    ye = _expert_ffn(xe, gates, w_gate, w_up, w_down)
    return _combine(ye.reshape(N_EXPERTS * cap, D), pos_t, s0, fast, x2d, ln_g, ln_b, cap=cap)


def _trunk(x, ret_w_in, ret_decay, ret_w_out, na_w_in, na_rpb, na_w_out,
           ln_mix_g, ln_mix_b, ln_ffn_g, ln_ffn_b, w_router, w_gate, w_up, w_down):
    B, S, D = x.shape
    x2d = x.reshape(B * S, D)
    cos, sin = _rotary_tables(S)
    for i in range(DEPTH):
        j = i // 2
        if i % 2 == 0:
            proj = _proj(x2d, ret_w_in[j], cos, sin, mode="ret", seq=S)
            y = _retention_core(proj, ret_decay[j], batch=B, seq=S)
            x2d, aff_t = _mix_out(y, proj, x2d, ret_w_out[j], ln_mix_g[i], ln_mix_b[i], w_router[i], ret=True)
        else:
            qkv = _proj(x2d, na_w_in[j], cos, sin, mode="na", seq=S)
            o = _na_core(qkv, na_rpb[j], batch=B, seq=S)
            x2d, aff_t = _mix_out(o, None, x2d, na_w_out[j], ln_mix_g[i], ln_mix_b[i], w_router[i], ret=False)
        x2d = _moe_ln(x2d, aff_t, w_gate[i], w_up[i], w_down[i], ln_ffn_g[i], ln_ffn_b[i])
    return x2d.reshape(B, S, D)


def kernel(x_prompt, x_sample, ret_w_in, ret_decay, ret_w_out, na_w_in, na_rpb, na_w_out, ln_mix_g, ln_mix_b,
           ln_ffn_g, ln_ffn_b, w_router, w_gate, w_up, w_down):
    weights = (ret_w_in, ret_decay, ret_w_out, na_w_in, na_rpb, na_w_out, ln_mix_g, ln_mix_b,
               ln_ffn_g, ln_ffn_b, w_router, w_gate, w_up, w_down)
    return _trunk(x_prompt, *weights), _trunk(x_sample, *weights)
```

```python
import functools

import jax
import jax.numpy as jnp
import numpy as np
from jax import lax
from jax.experimental import pallas as pl
from jax.experimental.pallas import tpu as pltpu

F32 = jnp.float32
BF16 = jnp.bfloat16

D_MODEL = 1024
DEPTH = 2
GRID_W = 64
RET_HEADS = 4
RET_DK = 256
RET_DV = 512
RET_QK_W = RET_HEADS * RET_DK
RET_V_W = RET_HEADS * RET_DV
ROPE_BASE = 10000.0
NA_HEADS = 32
NA_HD = D_MODEL // NA_HEADS
NA_KH = 8
NA_KW = 16
N_EXPERTS = 16
EXPERT_FF = 1024
CAPACITY_FACTOR = 2
DN_ALPHA = (2 * DEPTH) ** 0.25
LN_EPS = 1e-5
GN_EPS = 1e-6

VMEM_LIMIT_BYTES = 56 * 1024 * 1024
PROJ_COL_CHUNK = 1024
RET_CHUNK = 256
NA_ROWS_PER_STEP = 8
NA_HEAD_GROUP = 8
NA_QGROUP = 16
NA_WIN_COLS = 32
NA_WIN_STARTS = (0, 8, 24, 32)
NEG_BIG = -1e30
COMB_TOKENS = 256
COMB_WIN = 64
COMB_FAST_MAX = COMB_WIN - 8
COMB_SLOW_WIN = COMB_TOKENS + 8
COMB_SLOW_PAD = 384


def _cparams(sem):
    return pltpu.CompilerParams(dimension_semantics=sem, vmem_limit_bytes=VMEM_LIMIT_BYTES)


def _proj_kernel(x_ref, w_ref, cos_ref, sin_ref, o_ref, *, mode):
    xb = x_ref[...].astype(BF16)
    tn = PROJ_COL_CHUNK
    for c in range(w_ref.shape[1] // tn):
        acc = jnp.dot(xb, w_ref[:, c * tn:(c + 1) * tn], preferred_element_type=F32)
        if mode == "na" and c == 0:
            o_ref[:, 0:tn] = acc * (NA_HD ** -0.5)
        elif mode == "ret" and c < 2:
            scale = RET_DK ** -0.5 if c == 1 else 1.0
            cs = cos_ref[...]
            sn = sin_ref[...]
            half = RET_DK // 2
            for h in range(RET_HEADS):
                lo = c * tn + h * RET_DK
                x1 = acc[:, h * RET_DK:h * RET_DK + half] * scale
                x2 = acc[:, h * RET_DK + half:(h + 1) * RET_DK] * scale
                o_ref[:, lo:lo + half] = x1 * cs - x2 * sn
                o_ref[:, lo + half:lo + RET_DK] = x1 * sn + x2 * cs
        else:
            o_ref[:, c * tn:(c + 1) * tn] = acc


def _proj(x2d, w, cos, sin, *, mode, seq):
    T, K = x2d.shape
    N = w.shape[1]
    tm = 256
    nseq = seq // tm
    return pl.pallas_call(
        functools.partial(_proj_kernel, mode=mode),
        grid=(T // tm,),
        in_specs=[
            pl.BlockSpec((tm, K), lambda i: (i, 0)),
            pl.BlockSpec((K, N), lambda i: (0, 0)),
            pl.BlockSpec((tm, RET_DK // 2), lambda i: (i % nseq, 0)),
            pl.BlockSpec((tm, RET_DK // 2), lambda i: (i % nseq, 0)),
        ],
        out_specs=pl.BlockSpec((tm, N), lambda i: (i, 0)),
        out_shape=jax.ShapeDtypeStruct((T, N), F32),
        compiler_params=_cparams(("parallel",)),
        name="proj_" + mode,
    )(x2d, w.astype(BF16), cos, sin)


def _rotary_tables(seq):
    inv = 1.0 / (ROPE_BASE ** (jnp.arange(0, RET_DK, 2, dtype=F32) / RET_DK))
    ang = jnp.arange(seq, dtype=F32)[:, None] * inv[None, :]
    return jnp.cos(ang), jnp.sin(ang)


def _ret_kernel(cdec_ref, q_ref, k_ref, v_ref, dm_ref, cd_ref, sd_ref, o_ref, state_ref):
    d = pl.program_id(0)
    c = pl.program_id(2)

    @pl.when(c == 0)
    def _():
        state_ref[...] = jnp.zeros_like(state_ref)

    for h in range(RET_HEADS):
        q = q_ref[:, h * RET_DK:(h + 1) * RET_DK].astype(BF16)
        k = k_ref[:, h * RET_DK:(h + 1) * RET_DK]
        v = v_ref[:, h * RET_DV:(h + 1) * RET_DV].astype(BF16)
        scores = lax.dot_general(q, k.astype(BF16), (((1,), (1,)), ((), ())),
                                 preferred_element_type=F32) * dm_ref[h]
        o = jnp.dot(scores.astype(BF16), v, preferred_element_type=F32)
        state = state_ref[h]
        o = o + jnp.dot(q, state.astype(BF16), preferred_element_type=F32) * cd_ref[h]
        ks = (k * sd_ref[h]).astype(BF16)
        state_ref[h] = state * cdec_ref[d * RET_HEADS + h] + lax.dot_general(
            ks, v, (((0,), (0,)), ((), ())), preferred_element_type=F32)
        o_ref[:, h * RET_DV:(h + 1) * RET_DV] = o


def _ret_tables(decay_logit, C):
    log_g = jax.nn.log_sigmoid(decay_logit.astype(F32))
    i = jnp.arange(C, dtype=F32)
    diff = i[:, None] - i[None, :]
    lf = log_g[0][:, None, None]
    lb = log_g[1][:, None, None]
    dm_f = jnp.where((diff >= 0)[None], jnp.exp(lf * jnp.maximum(diff, 0.0)[None]), 0.0)
    dm_b = jnp.where((diff < 0)[None], jnp.exp(lb * jnp.maximum(-diff, 0.0)[None]), 0.0)
    cd_f = jnp.exp(log_g[0][:, None] * (i[None, :] + 1.0))
    cd_b = jnp.exp(log_g[1][:, None] * (C - i[None, :]))
    sd_f = jnp.exp(log_g[0][:, None] * (C - 1.0 - i[None, :]))
    sd_b = jnp.exp(log_g[1][:, None] * i[None, :])
    dm = jnp.stack([dm_f, dm_b]).astype(F32)
    cd = jnp.stack([cd_f, cd_b]).astype(F32)[..., None]
    sd = jnp.stack([sd_f, sd_b]).astype(F32)[..., None]
    cdec = jnp.exp(log_g * C).reshape(-1).astype(F32)
    return dm, cd, sd, cdec


def _retention_core(proj, decay_logit, *, batch, seq):
    T = proj.shape[0]
    C = RET_CHUNK
    NC = seq // C
    H = RET_HEADS
    dm, cd, sd, cdec = _ret_tables(decay_logit, C)

    def row(d, b, c):
        return b * NC + c + d * (NC - 1 - 2 * c)

    assert 2 * RET_QK_W == RET_V_W
    return pl.pallas_call(
        _ret_kernel,
        grid=(2, batch, NC),
        in_specs=[
            pl.BlockSpec(memory_space=pltpu.SMEM),
            pl.BlockSpec((C, RET_QK_W), lambda d, b, c: (row(d, b, c), 0)),
            pl.BlockSpec((C, RET_QK_W), lambda d, b, c: (row(d, b, c), 1)),
            pl.BlockSpec((C, RET_V_W), lambda d, b, c: (row(d, b, c), 1)),
            pl.BlockSpec((None, H, C, C), lambda d, b, c: (d, 0, 0, 0)),
            pl.BlockSpec((None, H, C, 1), lambda d, b, c: (d, 0, 0, 0)),
            pl.BlockSpec((None, H, C, 1), lambda d, b, c: (d, 0, 0, 0)),
        ],
        out_specs=pl.BlockSpec((None, C, RET_V_W), lambda d, b, c: (d, row(d, b, c), 0)),
        out_shape=jax.ShapeDtypeStruct((2, T, RET_V_W), F32),
        scratch_shapes=[pltpu.VMEM((H, RET_DK, RET_DV), F32)],
        compiler_params=_cparams(("parallel", "parallel", "arbitrary")),
        name="retention",
    )(cdec, proj, proj, proj, dm, cd, sd)


def _layer_norm(u, g, b):
    mu = jnp.mean(u, axis=-1, keepdims=True)
    var = jnp.mean(jnp.square(u - mu), axis=-1, keepdims=True)
    return (u - mu) * lax.rsqrt(var + LN_EPS) * g + b


def _mix_out_kernel(*refs, ret):
    if ret:
        y_ref, g_ref, x_ref, w_ref, lng_ref, lnb_ref, wr_ref, o_ref, aff_ref, wbf_ref = refs
    else:
        y_ref, x_ref, w_ref, lng_ref, lnb_ref, wr_ref, o_ref, aff_ref, wbf_ref = refs

    @pl.when(pl.program_id(0) == 0)
    def _():
        wbf_ref[...] = w_ref[...].astype(BF16)

    if ret:
        y = y_ref[0] + y_ref[1]
        parts = []
        for h in range(RET_HEADS):
            yh = y[:, h * RET_DV:(h + 1) * RET_DV]
            mu = jnp.mean(yh, axis=-1, keepdims=True)
            var = jnp.mean(jnp.square(yh - mu), axis=-1, keepdims=True)
            yn = (yh - mu) * lax.rsqrt(var + GN_EPS)
            gh = g_ref[:, h * RET_DV:(h + 1) * RET_DV]
            parts.append((gh * jax.nn.sigmoid(gh) * yn).astype(BF16))
        z = jnp.concatenate(parts, axis=1)
    else:
        z = y_ref[...].astype(BF16)
    mixed = jnp.dot(z, wbf_ref[...], preferred_element_type=F32)
    xn = _layer_norm(DN_ALPHA * x_ref[...] + mixed, lng_ref[...], lnb_ref[...])
    o_ref[...] = xn
    logits = lax.dot_general(wr_ref[...].astype(BF16), xn.astype(BF16), (((1,), (1,)), ((), ())),
                             preferred_element_type=F32)
    e = jnp.exp(logits - jnp.max(logits, axis=0, keepdims=True))
    aff_ref[...] = e / jnp.sum(e, axis=0, keepdims=True)


def _mix_out(y, gsrc, x2d, w_out, ln_g, ln_b, w_router, *, ret):
    T = x2d.shape[0]
    K = w_out.shape[0]
    tm = 256
    wr_t = w_router.T
    row = lambda i: (i, 0)
    const = lambda i: (0, 0)
    if ret:
        gcol = (2 * RET_QK_W + RET_V_W) // RET_V_W
        in_specs = [pl.BlockSpec((2, tm, K), lambda i: (0, i, 0)),
                    pl.BlockSpec((tm, K), lambda i: (i, gcol))]
        args = [y, gsrc]
    else:
        in_specs = [pl.BlockSpec((tm, K), row)]
        args = [y]
    in_specs += [pl.BlockSpec((tm, D_MODEL), row),
                 pl.BlockSpec((K, D_MODEL), const),
                 pl.BlockSpec((1, D_MODEL), const),
                 pl.BlockSpec((1, D_MODEL), const),
                 pl.BlockSpec((N_EXPERTS, D_MODEL), const)]
    args += [x2d, w_out, ln_g.reshape(1, -1), ln_b.reshape(1, -1), wr_t]
    return pl.pallas_call(
        functools.partial(_mix_out_kernel, ret=ret),
        grid=(T // tm,),
        in_specs=in_specs,
        out_specs=[pl.BlockSpec((tm, D_MODEL), row),
                   pl.BlockSpec((N_EXPERTS, tm), lambda i: (0, i))],
        out_shape=[jax.ShapeDtypeStruct((T, D_MODEL), F32),
                   jax.ShapeDtypeStruct((N_EXPERTS, T), F32)],
        scratch_shapes=[pltpu.VMEM((K, D_MODEL), BF16)],
        compiler_params=_cparams(("arbitrary",)),
        name="mix_out_ret" if ret else "mix_out_na",
    )(*args)


def _na_kernel(q_ref, k0_ref, k1_ref, k2_ref, v0_ref, v1_ref, v2_ref, bias_ref, o_ref,
               ka_ref, kb_ref, va_ref, vb_ref, *, rows):
    i = pl.program_id(2)
    R = NA_ROWS_PER_STEP
    lo, hi = NA_WIN_STARTS[1], NA_WIN_STARTS[2] + NA_WIN_COLS
    for t, (kr, vr) in enumerate(((k0_ref, v0_ref), (k1_ref, v1_ref), (k2_ref, v2_ref))):
        ka_ref[t * R:(t + 1) * R] = kr[...].astype(BF16)
        kb_ref[t * R:(t + 1) * R] = kr[:, lo:hi, :].astype(BF16)
        va_ref[t * R:(t + 1) * R] = vr[...].astype(BF16)
        vb_ref[t * R:(t + 1) * R] = vr[:, lo:hi, :].astype(BF16)

    nrow = NA_HEAD_GROUP * NA_QGROUP
    width = NA_HEAD_GROUP * NA_HD
    rid = lax.broadcasted_iota(jnp.int32, (nrow, width), 0) // NA_QGROUP
    cid = lax.broadcasted_iota(jnp.int32, (nrow, width), 1) // NA_HD
    head_mask = (rid == cid).astype(F32)
    nkeys = NA_KH * NA_WIN_COLS

    def body(rr, carry):
        r = i * R + rr
        rs = jnp.clip(r - NA_KH // 2, 0, rows - NA_KH)
        delta = r - rs
        rl = rs - (i * R - R)
        def window(a_ref, b_ref, j):
            c0 = NA_WIN_STARTS[j]
            if j in (0, 3):
                win = a_ref[pl.ds(rl, NA_KH), c0:c0 + NA_WIN_COLS, :]
            else:
                win = b_ref[pl.ds(rl, NA_KH), c0 - lo:c0 - lo + NA_WIN_COLS, :]
            return win.reshape(nkeys, width)

        scores = []
        for j in range(len(NA_WIN_STARTS)):
            qg = q_ref[rr, j * NA_QGROUP:(j + 1) * NA_QGROUP, :]
            lhs = (jnp.tile(qg, (NA_HEAD_GROUP, 1)) * head_mask).astype(BF16)
            scores.append(lax.dot_general(lhs, window(ka_ref, kb_ref, j), (((1,), (1,)), ((), ())),
                                          preferred_element_type=F32))
        s = jnp.concatenate(scores, axis=0) + bias_ref[delta].reshape(len(scores) * nrow, nkeys)
        e = jnp.exp(s - jnp.max(s, axis=-1, keepdims=True))
        p = (e * (1.0 / jnp.sum(e, axis=-1, keepdims=True))).astype(BF16)
        for j in range(len(NA_WIN_STARTS)):
            pv = jnp.dot(p[j * nrow:(j + 1) * nrow], window(va_ref, vb_ref, j),
                         preferred_element_type=F32) * head_mask
            out = pv[0:NA_QGROUP]
            for g in range(1, NA_HEAD_GROUP):
                out = out + pv[g * NA_QGROUP:(g + 1) * NA_QGROUP]
            o_ref[rr, j * NA_QGROUP:(j + 1) * NA_QGROUP, :] = out
        return carry

    lax.fori_loop(0, R, body, 0, unroll=True)


def _na_bias_table(rpb):
    HG = NA_HEADS // NA_HEAD_GROUP
    nj = GRID_W // NA_QGROUP
    delta = np.arange(NA_KH)[:, None]
    a = np.arange(NA_KH)[None, :]
    row_sel = np.eye(2 * NA_KH - 1, dtype=np.float32)[a - delta + (NA_KH - 1)]
    w = (np.arange(nj)[:, None] * NA_QGROUP + np.arange(NA_QGROUP)[None, :])[:, :, None]
    col = np.asarray(NA_WIN_STARTS)[:, None, None] + np.arange(NA_WIN_COLS)[None, None, :]
    cs = np.clip(w - NA_KW // 2, 0, GRID_W - NA_KW)
    valid = (col >= cs) & (col < cs + NA_KW)
    col_off = np.clip(col - w + (NA_KW - 1), 0, 2 * NA_KW - 2)
    col_sel = np.eye(2 * NA_KW - 1, dtype=np.float32)[col_off] * valid[..., None]
    rpb_g = rpb.astype(F32).reshape(HG, NA_HEAD_GROUP, 2 * NA_KH - 1, 2 * NA_KW - 1)
    vals = jnp.einsum("ghrc,dar,jqkc->gdjhqak", rpb_g, row_sel, col_sel, precision=lax.Precision.HIGHEST)
    vals = jnp.where(valid[None, None, :, None, :, None, :], vals, NEG_BIG)
    return vals.reshape(HG, NA_KH, nj, NA_HEAD_GROUP * NA_QGROUP, NA_KH * NA_WIN_COLS)


def _na_core(qkv, rpb, *, batch, seq):
    T = qkv.shape[0]
    rows = seq // GRID_W
    R = NA_ROWS_PER_STEP
    assert rows % R == 0 and rows >= NA_KH
    nb = rows // R
    HG = NA_HEADS // NA_HEAD_GROUP
    width = NA_HEAD_GROUP * NA_HD
    qkv3 = qkv.reshape(T // GRID_W, GRID_W, 3 * D_MODEL)
    bias = _na_bias_table(rpb)
    kcol0 = D_MODEL // width
    vcol0 = 2 * D_MODEL // width
    blk = (R, GRID_W, width)

    def kv_spec(col0, off):
        return pl.BlockSpec(blk, lambda g, b, i: (b * nb + jnp.clip(i + off, 0, nb - 1), 0, col0 + g))

    span = NA_WIN_STARTS[2] + NA_WIN_COLS - NA_WIN_STARTS[1]
    out = pl.pallas_call(
        functools.partial(_na_kernel, rows=rows),
        grid=(HG, batch, nb),
        in_specs=[pl.BlockSpec(blk, lambda g, b, i: (b * nb + i, 0, g)),
                  kv_spec(kcol0, -1), kv_spec(kcol0, 0), kv_spec(kcol0, 1),
                  kv_spec(vcol0, -1), kv_spec(vcol0, 0), kv_spec(vcol0, 1),
                  pl.BlockSpec((None,) + bias.shape[1:], lambda g, b, i: (g, 0, 0, 0, 0))],
        out_specs=pl.BlockSpec(blk, lambda g, b, i: (b * nb + i, 0, g)),
        out_shape=jax.ShapeDtypeStruct((T // GRID_W, GRID_W, D_MODEL), F32),
        scratch_shapes=[pltpu.VMEM((3 * R, GRID_W, width), BF16),
                        pltpu.VMEM((3 * R, span, width), BF16),
                        pltpu.VMEM((3 * R, GRID_W, width), BF16),
                        pltpu.VMEM((3 * R, span, width), BF16)],
        compiler_params=_cparams(("parallel", "parallel", "parallel")),
        name="natten",
    )(qkv3, qkv3, qkv3, qkv3, qkv3, qkv3, qkv3, bias)
    return out.reshape(T, D_MODEL)


def _ffn_kernel(x_ref, gate_ref, wg_ref, wu_ref, wd_ref, o_ref, wgb_ref, wub_ref, wdb_ref):
    @pl.when(pl.program_id(1) == 0)
    def _():
        wgb_ref[...] = wg_ref[...].astype(BF16)
        wub_ref[...] = wu_ref[...].astype(BF16)
        wdb_ref[...] = wd_ref[...].astype(BF16)

    x = x_ref[...].astype(BF16)
    a = jnp.dot(x, wgb_ref[...], preferred_element_type=F32)
    b = jnp.dot(x, wub_ref[...], preferred_element_type=F32)
    hidden = (a * jax.nn.sigmoid(a) * b).astype(BF16)
    o_ref[...] = jnp.dot(hidden, wdb_ref[...], preferred_element_type=F32) * gate_ref[...]


def _expert_ffn(xe, gates, w_gate, w_up, w_down):
    E, cap, D = xe.shape
    Fh = w_gate.shape[-1]
    tm = 512
    wspec = lambda a, b: pl.BlockSpec((None, a, b), lambda e, i: (e, 0, 0))
    return pl.pallas_call(
        _ffn_kernel,
        grid=(E, cap // tm),
        in_specs=[pl.BlockSpec((None, tm, D), lambda e, i: (e, i, 0)),
                  pl.BlockSpec((None, tm, 1), lambda e, i: (e, i, 0)),
                  wspec(D, Fh), wspec(D, Fh), wspec(Fh, D)],
        out_specs=pl.BlockSpec((None, tm, D), lambda e, i: (e, i, 0)),
        out_shape=jax.ShapeDtypeStruct((E, cap, D), F32),
        scratch_shapes=[pltpu.VMEM((D, Fh), BF16), pltpu.VMEM((D, Fh), BF16), pltpu.VMEM((Fh, D), BF16)],
        compiler_params=_cparams(("parallel", "arbitrary")),
        name="expert_ffn",
    )(xe, gates[..., None], w_gate, w_up, w_down)


def _split_dot(p, rows):
    hi = rows.astype(BF16)
    lo = (rows - hi.astype(F32)).astype(BF16)
    return jnp.dot(p, hi, preferred_element_type=F32) + jnp.dot(p, lo, preferred_element_type=F32)


def _combine_kernel(s0_ref, fast_ref, pos_ref, x_ref, g_ref, b_ref, ye_ref, o_ref,
                    stage_ref, slow_ref, f_ref, sem_ref, slow_sem, *, nblk, cap):
    i = pl.program_id(0)
    slot = i % 2
    E = N_EXPERTS
    TB = COMB_TOKENS

    def first_slot(blk, e):
        return s0_ref[e * (nblk + 1) + blk]

    def window_base(blk, e, win):
        return jnp.minimum((first_slot(blk, e) // 8) * 8, cap - win)

    def window_copy(blk, e, sl):
        src = pl.multiple_of(e * cap + window_base(blk, e, COMB_WIN), 8)
        return pltpu.make_async_copy(ye_ref.at[pl.ds(src, COMB_WIN)],
                                     stage_ref.at[sl, pl.ds(e * COMB_WIN, COMB_WIN)], sem_ref.at[sl])

    @pl.when(i == 0)
    def _():
        slow_ref[...] = jnp.zeros_like(slow_ref)
        for e in range(E):
            window_copy(0, e, 0).start()

    @pl.when(i + 1 < nblk)
    def _():
        for e in range(E):
            window_copy(i + 1, e, 1 - slot).start()

    for e in range(E):
        window_copy(i, e, slot).wait()

    pos = pos_ref[...]
    lane_e = lax.broadcasted_iota(jnp.int32, (TB, E), 1)

    @pl.when(fast_ref[i] == 1)
    def _():
        base = jnp.zeros((1, E), jnp.int32)
        for e in range(E):
            base = jnp.where(lane_e[0:1] == e, window_base(i, e, COMB_WIN), base)
        kidx = jnp.where(pos >= 0, pos - base, -1)
        lane = lax.broadcasted_iota(jnp.int32, (TB, 2 * COMB_WIN), 1)
        cols = []
        for c in range(E // 2):
            k0 = kidx[:, 2 * c:2 * c + 1]
            k1 = kidx[:, 2 * c + 1:2 * c + 2]
            k1 = jnp.where(k1 >= 0, k1 + COMB_WIN, -1)
            cols.append(jnp.where((lane == k0) | (lane == k1), 1.0, 0.0).astype(BF16))
        f_ref[...] = _split_dot(jnp.concatenate(cols, axis=1), stage_ref[slot])

    @pl.when(fast_ref[i] == 0)
    def _():
        lane = lax.broadcasted_iota(jnp.int32, (TB, COMB_SLOW_PAD), 1)

        def body(e, acc):
            base = window_base(i, e, COMB_SLOW_WIN)
            src = pl.multiple_of(e * cap + base, 8)
            cp = pltpu.make_async_copy(ye_ref.at[pl.ds(src, COMB_SLOW_WIN)],
                                       slow_ref.at[pl.ds(0, COMB_SLOW_WIN)], slow_sem)
            cp.start()
            cp.wait()
            col = jnp.max(jnp.where(lane_e == e, pos, -1), axis=1, keepdims=True)
            k = jnp.where(col >= 0, col - base, -1)
            p = jnp.where(lane == k, 1.0, 0.0).astype(BF16)
            return acc + _split_dot(p, slow_ref[...])

        f_ref[...] = lax.fori_loop(0, E, body, jnp.zeros(f_ref.shape, F32))

    o_ref[...] = _layer_norm(DN_ALPHA * x_ref[...] + f_ref[...], g_ref[...], b_ref[...])


def _combine(ye2d, pos_t, s0, fast, x2d, ln_g, ln_b, *, cap):
    T, D = x2d.shape
    E = N_EXPERTS
    TB = COMB_TOKENS
    nblk = T // TB
    assert cap >= COMB_SLOW_WIN and cap % 8 == 0 and T % TB == 0
    row = lambda i, s0, fl: (i, 0)
    const = lambda i, s0, fl: (0, 0)
    return pl.pallas_call(
        functools.partial(_combine_kernel, nblk=nblk, cap=cap),
        grid_spec=pltpu.PrefetchScalarGridSpec(
            num_scalar_prefetch=2,
            grid=(nblk,),
            in_specs=[pl.BlockSpec((TB, E), row), pl.BlockSpec((TB, D), row),
                      pl.BlockSpec((1, D), const), pl.BlockSpec((1, D), const),
                      pl.BlockSpec(memory_space=pl.ANY)],
            out_specs=pl.BlockSpec((TB, D), row),
            scratch_shapes=[pltpu.VMEM((2, E * COMB_WIN, D), F32),
                            pltpu.VMEM((COMB_SLOW_PAD, D), F32),
                            pltpu.VMEM((TB, D), F32),
                            pltpu.SemaphoreType.DMA((2,)),
                            pltpu.SemaphoreType.DMA(())]),
        out_shape=jax.ShapeDtypeStruct((T, D), F32),
        compiler_params=_cparams(("arbitrary",)),
        name="combine_ln",
    )(s0, fast, pos_t, x2d, ln_g.reshape(1, -1), ln_b.reshape(1, -1), ye2d)


def _select_kernel(aff_ref, u_ref, l_ref, pos_ref, off_ref, *, cap):
    E, R, L = aff_ref.shape
    bits = lax.bitcast_convert_type(aff_ref[...], jnp.int32)

    def count(mask):
        ones = jnp.where(mask, 1.0, 0.0)
        return jnp.sum(jnp.sum(ones, axis=2, keepdims=True), axis=1, keepdims=True)

    def step(b, prefix):
        cand = prefix | jnp.left_shift(jnp.int32(1), 30 - b)
        return jnp.where(count(bits >= cand) >= cap, cand, prefix)

    tau = lax.fori_loop(0, 31, step, jnp.zeros((E, 1, 1), jnp.int32))

    def cumsum(mask):
        x2 = jnp.where(mask, 1.0, 0.0).astype(BF16).reshape(E * R, L)
        within = jnp.dot(x2, u_ref[...], preferred_element_type=F32).reshape(E, R, L)
        tot = jnp.broadcast_to(within[:, :, L - 1:L], (E, R, L)).astype(BF16)
        before = jnp.stack([jnp.dot(l_ref[...], tot[e], preferred_element_type=F32) for e in range(E)])
        return within + before, before

    above = bits > tau
    tied = bits == tau
    need = cap - count(above)
    tied_rank, _ = cumsum(tied)
    sel = above | (tied & (tied_rank <= need))
    csum, before = cumsum(sel)
    pos_ref[...] = jnp.where(sel, csum - 1.0, -1.0).astype(jnp.int32)
    off_ref[...] = before.astype(jnp.int32)


def _compact_kernel(off_ref, pos_ref, aff_ref, acc_ref, *, rows, cap):
    n_rows, L = pos_ref.shape
    nchunk = cap // L
    acc_ref[...] = jnp.zeros_like(acc_ref)
    slot_in_chunk = lax.broadcasted_iota(jnp.int32, (L, L), 0)
    field = lax.broadcasted_iota(jnp.int32, (8, L), 0)
    lane = lax.broadcasted_iota(jnp.int32, (8, L), 1).astype(F32)

    def body(n, carry):
        e = n // rows
        r = n - e * rows
        chunk = off_ref[n] // L
        rel = pos_ref[pl.ds(n, 1), :] - chunk * L
        g = aff_ref[pl.ds(n, 1), :]
        g1 = g.astype(BF16).astype(F32)
        g2 = (g - g1).astype(BF16).astype(F32)
        g3 = g - g1 - g2
        vals = jnp.where(field == 0, lane,
                         jnp.where(field == 1, 1.0,
                                   jnp.where(field == 2, g1,
                                             jnp.where(field == 3, g2,
                                                       jnp.where(field == 4, g3, 0.0))))).astype(BF16)
        tok0 = (r * L).astype(F32)
        for half in range(2):
            hit = jnp.where(rel == slot_in_chunk + half * L, 1.0, 0.0).astype(BF16)
            res = lax.dot_general(vals, hit, (((1,), (1,)), ((), ())), preferred_element_type=F32)
            res = res + jnp.where(field == 0, tok0 * res[1:2, :], 0.0)
            acc_ref[e * nchunk + chunk + half] += res
        return carry

    lax.fori_loop(0, n_rows, body, 0, unroll=8)


def _route(aff_t, cap):
    E, T = aff_t.shape
    L = 128
    R = T // L
    assert T % L == 0 and cap % L == 0 and R % 16 == 0
    u = jnp.asarray(np.arange(L)[:, None] <= np.arange(L)[None, :], BF16)
    low = jnp.asarray(np.arange(R)[None, :] < np.arange(R)[:, None], BF16)
    aff3 = aff_t.reshape(E, R, L)
    full = lambda shape: pl.BlockSpec(shape, lambda i: (0,) * len(shape))
    pos3, before3 = pl.pallas_call(
        functools.partial(_select_kernel, cap=cap),
        grid=(1,),
        in_specs=[full((E, R, L)), full((L, L)), full((R, R))],
        out_specs=[full((E, R, L)), full((E, R, L))],
        out_shape=[jax.ShapeDtypeStruct((E, R, L), jnp.int32)] * 2,
        compiler_params=_cparams(("arbitrary",)),
        name="route_select",
    )(aff3, u, low)
    row_off = before3[:, :, 0]
    nchunk = cap // L
    acc = pl.pallas_call(
        functools.partial(_compact_kernel, rows=R, cap=cap),
        grid_spec=pltpu.PrefetchScalarGridSpec(
            num_scalar_prefetch=1,
            grid=(1,),
            in_specs=[pl.BlockSpec((E * R, L), lambda i, off: (0, 0)),
                      pl.BlockSpec((E * R, L), lambda i, off: (0, 0))],
            out_specs=pl.BlockSpec((E * nchunk + 1, 8, L), lambda i, off: (0, 0, 0))),
        out_shape=jax.ShapeDtypeStruct((E * nchunk + 1, 8, L), F32),
        compiler_params=_cparams(("arbitrary",)),
        name="route_compact",
    )(row_off.reshape(-1), pos3.reshape(E * R, L), aff3.reshape(E * R, L))
    acc = acc[:E * nchunk]
    idx = acc[:, 0, :].astype(jnp.int32).reshape(E, cap)
    gates = ((acc[:, 2, :] + acc[:, 3, :]) + acc[:, 4, :]).reshape(E, cap)
    pos_t = pos3.reshape(E, T).T
    s0 = jnp.concatenate([row_off[:, ::COMB_TOKENS // L], jnp.full((E, 1), cap, jnp.int32)], axis=1)
    fast = jnp.all(s0[:, 1:] - s0[:, :-1] <= COMB_FAST_MAX, axis=0).astype(jnp.int32)
    return idx, gates, pos_t, s0.reshape(-1), fast


def _moe_ln(x2d, aff_t, w_gate, w_up, w_down, ln_g, ln_b):
    T, D = x2d.shape
    cap = CAPACITY_FACTOR * T // N_EXPERTS
    idx, gates, pos_t, s0, fast = _route(aff_t, cap)
    xe = x2d.at[idx].get(mode="promise_in_bounds")
    ye = _expert_ffn(xe, gates, w_gate, w_up, w_down)
    return _combine(ye.reshape(N_EXPERTS * cap, D), pos_t, s0, fast, x2d, ln_g, ln_b, cap=cap)


def _trunk(x, ret_w_in, ret_decay, ret_w_out, na_w_in, na_rpb, na_w_out,
           ln_mix_g, ln_mix_b, ln_ffn_g, ln_ffn_b, w_router, w_gate, w_up, w_down):
    B, S, D = x.shape
    x2d = x.reshape(B * S, D)
    cos, sin = _rotary_tables(S)
    for i in range(DEPTH):
        j = i // 2
        if i % 2 == 0:
            proj = _proj(x2d, ret_w_in[j], cos, sin, mode="ret", seq=S)
            y = _retention_core(proj, ret_decay[j], batch=B, seq=S)
            x2d, aff_t = _mix_out(y, proj, x2d, ret_w_out[j], ln_mix_g[i], ln_mix_b[i], w_router[i], ret=True)
        else:
            qkv = _proj(x2d, na_w_in[j], cos, sin, mode="na", seq=S)
            o = _na_core(qkv, na_rpb[j], batch=B, seq=S)
            x2d, aff_t = _mix_out(o, None, x2d, na_w_out[j], ln_mix_g[i], ln_mix_b[i], w_router[i], ret=False)
        x2d = _moe_ln(x2d, aff_t, w_gate[i], w_up[i], w_down[i], ln_ffn_g[i], ln_ffn_b[i])
    return x2d.reshape(B, S, D)


def kernel(x_prompt, x_sample, ret_w_in, ret_decay, ret_w_out, na_w_in, na_rpb, na_w_out, ln_mix_g, ln_mix_b,
           ln_ffn_g, ln_ffn_b, w_router, w_gate, w_up, w_down):
    weights = (ret_w_in, ret_decay, ret_w_out, na_w_in, na_rpb, na_w_out, ln_mix_g, ln_mix_b,
               ln_ffn_g, ln_ffn_b, w_router, w_gate, w_up, w_down)
    return _trunk(x_prompt, *weights), _trunk(x_sample, *weights)
```

```python
import functools

import jax
import jax.numpy as jnp
import numpy as np
from jax import lax
from jax.experimental import pallas as pl
from jax.experimental.pallas import tpu as pltpu

F32 = jnp.float32
BF16 = jnp.bfloat16

D_MODEL = 1024
DEPTH = 2
GRID_W = 64
RET_HEADS = 4
RET_DK = 256
RET_DV = 512
RET_QK_W = RET_HEADS * RET_DK
RET_V_W = RET_HEADS * RET_DV
ROPE_BASE = 10000.0
NA_HEADS = 32
NA_HD = D_MODEL // NA_HEADS
NA_KH = 8
NA_KW = 16
N_EXPERTS = 16
EXPERT_FF = 1024
CAPACITY_FACTOR = 2
DN_ALPHA = (2 * DEPTH) ** 0.25
LN_EPS = 1e-5
GN_EPS = 1e-6

VMEM_LIMIT_BYTES = 56 * 1024 * 1024
PROJ_COL_CHUNK = 1024
RET_CHUNK = 256
NA_ROWS_PER_STEP = 8
NA_HEAD_GROUP = 8
NA_QGROUP = 16
NA_WIN_COLS = 32
NA_WIN_STARTS = (0, 8, 24, 32)
NEG_BIG = -1e30
COMB_TOKENS = 256
COMB_WIN = 64
COMB_FAST_MAX = COMB_WIN - 8
COMB_SLOW_WIN = COMB_TOKENS + 8
COMB_SLOW_PAD = 384


def _cparams(sem):
    return pltpu.CompilerParams(dimension_semantics=sem, vmem_limit_bytes=VMEM_LIMIT_BYTES)


def _proj_kernel(x_ref, w_ref, cos_ref, sin_ref, *o_refs, mode):
    xb = x_ref[...].astype(BF16)
    tn = PROJ_COL_CHUNK

    def chunk(c):
        return jnp.dot(xb, w_ref[:, c * tn:(c + 1) * tn], preferred_element_type=F32)

    if mode == "na":
        (o_ref,) = o_refs
        for c in range(w_ref.shape[1] // tn):
            acc = chunk(c)
            o_ref[:, c * tn:(c + 1) * tn] = acc * (NA_HD ** -0.5) if c == 0 else acc
    else:
        q_ref, k_ref, v_ref, g_ref = o_refs
        cs = cos_ref[...]
        sn = sin_ref[...]
        half = RET_DK // 2
        for c, dst, scale in ((0, q_ref, 1.0), (1, k_ref, RET_DK ** -0.5)):
            acc = chunk(c)
            for h in range(RET_HEADS):
                x1 = acc[:, h * RET_DK:h * RET_DK + half] * scale
                x2 = acc[:, h * RET_DK + half:(h + 1) * RET_DK] * scale
                dst[:, h * RET_DK:h * RET_DK + half] = (x1 * cs - x2 * sn).astype(dst.dtype)
                dst[:, h * RET_DK + half:(h + 1) * RET_DK] = (x1 * sn + x2 * cs).astype(dst.dtype)
        for c in range(2):
            v_ref[:, c * tn:(c + 1) * tn] = chunk(2 + c).astype(v_ref.dtype)
            g_ref[:, c * tn:(c + 1) * tn] = chunk(4 + c)


def _proj(x2d, w, cos, sin, *, mode, seq):
    T, K = x2d.shape
    N = w.shape[1]
    tm = 256
    nseq = seq // tm
    row = lambda i: (i, 0)
    if mode == "na":
        widths, dtypes = (N,), (F32,)
    else:
        assert N == 2 * RET_QK_W + 2 * RET_V_W and RET_QK_W == PROJ_COL_CHUNK and RET_V_W == 2 * PROJ_COL_CHUNK
        widths, dtypes = (RET_QK_W, RET_QK_W, RET_V_W, RET_V_W), (BF16, F32, BF16, F32)
    outs = pl.pallas_call(
        functools.partial(_proj_kernel, mode=mode),
        grid=(T // tm,),
        in_specs=[
            pl.BlockSpec((tm, K), row),
            pl.BlockSpec((K, N), lambda i: (0, 0)),
            pl.BlockSpec((tm, RET_DK // 2), lambda i: (i % nseq, 0)),
            pl.BlockSpec((tm, RET_DK // 2), lambda i: (i % nseq, 0)),
        ],
        out_specs=[pl.BlockSpec((tm, n), row) for n in widths],
        out_shape=[jax.ShapeDtypeStruct((T, n), dt) for n, dt in zip(widths, dtypes)],
        compiler_params=_cparams(("parallel",)),
        name="proj_" + mode,
    )(x2d, w.astype(BF16), cos, sin)
    return outs[0] if mode == "na" else outs


def _rotary_tables(seq):
    inv = 1.0 / (ROPE_BASE ** (jnp.arange(0, RET_DK, 2, dtype=F32) / RET_DK))
    ang = jnp.arange(seq, dtype=F32)[:, None] * inv[None, :]
    return jnp.cos(ang), jnp.sin(ang)


def _ret_kernel(cdec_ref, q_ref, k_ref, v_ref, dm_ref, cd_ref, sd_ref, o_ref, state_ref):
    d = pl.program_id(0)
    c = pl.program_id(2)

    @pl.when(c == 0)
    def _():
        state_ref[...] = jnp.zeros_like(state_ref)

    for h in range(RET_HEADS):
        q = q_ref[:, h * RET_DK:(h + 1) * RET_DK].astype(BF16)
        k = k_ref[:, h * RET_DK:(h + 1) * RET_DK]
        v = v_ref[:, h * RET_DV:(h + 1) * RET_DV].astype(BF16)
        scores = lax.dot_general(q, k.astype(BF16), (((1,), (1,)), ((), ())),
                                 preferred_element_type=F32) * dm_ref[h]
        o = jnp.dot(scores.astype(BF16), v, preferred_element_type=F32)
        state = state_ref[h]
        o = o + jnp.dot(q, state.astype(BF16), preferred_element_type=F32) * cd_ref[h]
        ks = (k * sd_ref[h]).astype(BF16)
        state_ref[h] = state * cdec_ref[d * RET_HEADS + h] + lax.dot_general(
            ks, v, (((0,), (0,)), ((), ())), preferred_element_type=F32)
        o_ref[:, h * RET_DV:(h + 1) * RET_DV] = o


def _ret_tables(decay_logit, C):
    log_g = jax.nn.log_sigmoid(decay_logit.astype(F32))
    i = jnp.arange(C, dtype=F32)
    diff = i[:, None] - i[None, :]
    lf = log_g[0][:, None, None]
    lb = log_g[1][:, None, None]
    dm_f = jnp.where((diff >= 0)[None], jnp.exp(lf * jnp.maximum(diff, 0.0)[None]), 0.0)
    dm_b = jnp.where((diff < 0)[None], jnp.exp(lb * jnp.maximum(-diff, 0.0)[None]), 0.0)
    cd_f = jnp.exp(log_g[0][:, None] * (i[None, :] + 1.0))
    cd_b = jnp.exp(log_g[1][:, None] * (C - i[None, :]))
    sd_f = jnp.exp(log_g[0][:, None] * (C - 1.0 - i[None, :]))
    sd_b = jnp.exp(log_g[1][:, None] * i[None, :])
    dm = jnp.stack([dm_f, dm_b]).astype(F32)
    cd = jnp.stack([cd_f, cd_b]).astype(F32)[..., None]
    sd = jnp.stack([sd_f, sd_b]).astype(F32)[..., None]
    cdec = jnp.exp(log_g * C).reshape(-1).astype(F32)
    return dm, cd, sd, cdec


def _retention_core(q, k, v, decay_logit, *, batch, seq):
    T = q.shape[0]
    C = RET_CHUNK
    NC = seq // C
    H = RET_HEADS
    dm, cd, sd, cdec = _ret_tables(decay_logit, C)

    def row(d, b, c):
        return b * NC + c + d * (NC - 1 - 2 * c)

    return pl.pallas_call(
        _ret_kernel,
        grid=(2, batch, NC),
        in_specs=[
            pl.BlockSpec(memory_space=pltpu.SMEM),
            pl.BlockSpec((C, RET_QK_W), lambda d, b, c: (row(d, b, c), 0)),
            pl.BlockSpec((C, RET_QK_W), lambda d, b, c: (row(d, b, c), 0)),
            pl.BlockSpec((C, RET_V_W), lambda d, b, c: (row(d, b, c), 0)),
            pl.BlockSpec((None, H, C, C), lambda d, b, c: (d, 0, 0, 0)),
            pl.BlockSpec((None, H, C, 1), lambda d, b, c: (d, 0, 0, 0)),
            pl.BlockSpec((None, H, C, 1), lambda d, b, c: (d, 0, 0, 0)),
        ],
        out_specs=pl.BlockSpec((None, C, RET_V_W), lambda d, b, c: (d, row(d, b, c), 0)),
        out_shape=jax.ShapeDtypeStruct((2, T, RET_V_W), F32),
        scratch_shapes=[pltpu.VMEM((H, RET_DK, RET_DV), F32)],
        compiler_params=_cparams(("parallel", "parallel", "arbitrary")),
        name="retention",
    )(cdec, q, k, v, dm, cd, sd)


def _layer_norm(u, g, b):
    mu = jnp.mean(u, axis=-1, keepdims=True)
    var = jnp.mean(jnp.square(u - mu), axis=-1, keepdims=True)
    return (u - mu) * lax.rsqrt(var + LN_EPS) * g + b


def _mix_out_kernel(*refs, ret):
    if ret:
        y_ref, g_ref, x_ref, w_ref, lng_ref, lnb_ref, wr_ref, o_ref, aff_ref, wbf_ref = refs
    else:
        y_ref, x_ref, w_ref, lng_ref, lnb_ref, wr_ref, o_ref, aff_ref, wbf_ref = refs

    @pl.when(pl.program_id(0) == 0)
    def _():
        wbf_ref[...] = w_ref[...].astype(BF16)

    if ret:
        y = y_ref[0] + y_ref[1]
        parts = []
        for h in range(RET_HEADS):
            yh = y[:, h * RET_DV:(h + 1) * RET_DV]
            mu = jnp.mean(yh, axis=-1, keepdims=True)
            var = jnp.mean(jnp.square(yh - mu), axis=-1, keepdims=True)
            yn = (yh - mu) * lax.rsqrt(var + GN_EPS)
            gh = g_ref[:, h * RET_DV:(h + 1) * RET_DV]
            parts.append((gh * jax.nn.sigmoid(gh) * yn).astype(BF16))
        z = jnp.concatenate(parts, axis=1)
    else:
        z = y_ref[...].astype(BF16)
    mixed = jnp.dot(z, wbf_ref[...], preferred_element_type=F32)
    xn = _layer_norm(DN_ALPHA * x_ref[...] + mixed, lng_ref[...], lnb_ref[...])
    o_ref[...] = xn
    logits = lax.dot_general(wr_ref[...].astype(BF16), xn.astype(BF16), (((1,), (1,)), ((), ())),
                             preferred_element_type=F32)
    e = jnp.exp(logits - jnp.max(logits, axis=0, keepdims=True))
    aff_ref[...] = e / jnp.sum(e, axis=0, keepdims=True)


def _mix_out(y, gsrc, x2d, w_out, ln_g, ln_b, w_router, *, ret):
    T = x2d.shape[0]
    K = w_out.shape[0]
    tm = 256
    wr_t = w_router.T
    row = lambda i: (i, 0)
    const = lambda i: (0, 0)
    if ret:
        in_specs = [pl.BlockSpec((2, tm, K), lambda i: (0, i, 0)),
                    pl.BlockSpec((tm, K), row)]
        args = [y, gsrc]
    else:
        in_specs = [pl.BlockSpec((tm, K), row)]
        args = [y]
    in_specs += [pl.BlockSpec((tm, D_MODEL), row),
                 pl.BlockSpec((K, D_MODEL), const),
                 pl.BlockSpec((1, D_MODEL), const),
                 pl.BlockSpec((1, D_MODEL), const),
                 pl.BlockSpec((N_EXPERTS, D_MODEL), const)]
    args += [x2d, w_out, ln_g.reshape(1, -1), ln_b.reshape(1, -1), wr_t]
    return pl.pallas_call(
        functools.partial(_mix_out_kernel, ret=ret),
        grid=(T // tm,),
        in_specs=in_specs,
        out_specs=[pl.BlockSpec((tm, D_MODEL), row),
                   pl.BlockSpec((N_EXPERTS, tm), lambda i: (0, i))],
        out_shape=[jax.ShapeDtypeStruct((T, D_MODEL), F32),
                   jax.ShapeDtypeStruct((N_EXPERTS, T), F32)],
        scratch_shapes=[pltpu.VMEM((K, D_MODEL), BF16)],
        compiler_params=_cparams(("arbitrary",)),
        name="mix_out_ret" if ret else "mix_out_na",
    )(*args)


def _na_kernel(q_ref, k0_ref, k1_ref, k2_ref, v0_ref, v1_ref, v2_ref, bias_ref, o_ref,
               ka_ref, kb_ref, va_ref, vb_ref, *, rows):
    i = pl.program_id(2)
    R = NA_ROWS_PER_STEP
    lo, hi = NA_WIN_STARTS[1], NA_WIN_STARTS[2] + NA_WIN_COLS
    for t, (kr, vr) in enumerate(((k0_ref, v0_ref), (k1_ref, v1_ref), (k2_ref, v2_ref))):
        ka_ref[t * R:(t + 1) * R] = kr[...].astype(BF16)
        kb_ref[t * R:(t + 1) * R] = kr[:, lo:hi, :].astype(BF16)
        va_ref[t * R:(t + 1) * R] = vr[...].astype(BF16)
        vb_ref[t * R:(t + 1) * R] = vr[:, lo:hi, :].astype(BF16)

    nrow = NA_HEAD_GROUP * NA_QGROUP
    width = NA_HEAD_GROUP * NA_HD
    rid = lax.broadcasted_iota(jnp.int32, (nrow, width), 0) // NA_QGROUP
    cid = lax.broadcasted_iota(jnp.int32, (nrow, width), 1) // NA_HD
    head_mask = (rid == cid).astype(F32)
    nkeys = NA_KH * NA_WIN_COLS

    def body(rr, carry):
        r = i * R + rr
        rs = jnp.clip(r - NA_KH // 2, 0, rows - NA_KH)
        delta = r - rs
        rl = rs - (i * R - R)
        def window(a_ref, b_ref, j):
            c0 = NA_WIN_STARTS[j]
            if j in (0, 3):
                win = a_ref[pl.ds(rl, NA_KH), c0:c0 + NA_WIN_COLS, :]
            else:
                win = b_ref[pl.ds(rl, NA_KH), c0 - lo:c0 - lo + NA_WIN_COLS, :]
            return win.reshape(nkeys, width)

        scores = []
        for j in range(len(NA_WIN_STARTS)):
            qg = q_ref[rr, j * NA_QGROUP:(j + 1) * NA_QGROUP, :]
            lhs = (jnp.tile(qg, (NA_HEAD_GROUP, 1)) * head_mask).astype(BF16)
            scores.append(lax.dot_general(lhs, window(ka_ref, kb_ref, j), (((1,), (1,)), ((), ())),
                                          preferred_element_type=F32))
        s = jnp.concatenate(scores, axis=0) + bias_ref[delta].reshape(len(scores) * nrow, nkeys)
        e = jnp.exp(s - jnp.max(s, axis=-1, keepdims=True))
        p = (e * (1.0 / jnp.sum(e, axis=-1, keepdims=True))).astype(BF16)
        for j in range(len(NA_WIN_STARTS)):
            pv = jnp.dot(p[j * nrow:(j + 1) * nrow], window(va_ref, vb_ref, j),
                         preferred_element_type=F32) * head_mask
            out = pv[0:NA_QGROUP]
            for g in range(1, NA_HEAD_GROUP):
                out = out + pv[g * NA_QGROUP:(g + 1) * NA_QGROUP]
            o_ref[rr, j * NA_QGROUP:(j + 1) * NA_QGROUP, :] = out
        return carry

    lax.fori_loop(0, R, body, 0, unroll=True)


def _na_bias_table(rpb):
    HG = NA_HEADS // NA_HEAD_GROUP
    nj = GRID_W // NA_QGROUP
    delta = np.arange(NA_KH)[:, None]
    a = np.arange(NA_KH)[None, :]
    row_sel = np.eye(2 * NA_KH - 1, dtype=np.float32)[a - delta + (NA_KH - 1)]
    w = (np.arange(nj)[:, None] * NA_QGROUP + np.arange(NA_QGROUP)[None, :])[:, :, None]
    col = np.asarray(NA_WIN_STARTS)[:, None, None] + np.arange(NA_WIN_COLS)[None, None, :]
    cs = np.clip(w - NA_KW // 2, 0, GRID_W - NA_KW)
    valid = (col >= cs) & (col < cs + NA_KW)
    col_off = np.clip(col - w + (NA_KW - 1), 0, 2 * NA_KW - 2)
    col_sel = np.eye(2 * NA_KW - 1, dtype=np.float32)[col_off] * valid[..., None]
    rpb_g = rpb.astype(F32).reshape(HG, NA_HEAD_GROUP, 2 * NA_KH - 1, 2 * NA_KW - 1)
    vals = jnp.einsum("ghrc,dar,jqkc->gdjhqak", rpb_g, row_sel, col_sel, precision=lax.Precision.HIGHEST)
    vals = jnp.where(valid[None, None, :, None, :, None, :], vals, NEG_BIG)
    return vals.reshape(HG, NA_KH, nj, NA_HEAD_GROUP * NA_QGROUP, NA_KH * NA_WIN_COLS)


def _na_core(qkv, rpb, *, batch, seq):
    T = qkv.shape[0]
    rows = seq // GRID_W
    R = NA_ROWS_PER_STEP
    assert rows % R == 0 and rows >= NA_KH
    nb = rows // R
    HG = NA_HEADS // NA_HEAD_GROUP
    width = NA_HEAD_GROUP * NA_HD
    qkv3 = qkv.reshape(T // GRID_W, GRID_W, 3 * D_MODEL)
    bias = _na_bias_table(rpb)
    kcol0 = D_MODEL // width
    vcol0 = 2 * D_MODEL // width
    blk = (R, GRID_W, width)

    def kv_spec(col0, off):
        return pl.BlockSpec(blk, lambda g, b, i: (b * nb + jnp.clip(i + off, 0, nb - 1), 0, col0 + g))

    span = NA_WIN_STARTS[2] + NA_WIN_COLS - NA_WIN_STARTS[1]
    out = pl.pallas_call(
        functools.partial(_na_kernel, rows=rows),
        grid=(HG, batch, nb),
        in_specs=[pl.BlockSpec(blk, lambda g, b, i: (b * nb + i, 0, g)),
                  kv_spec(kcol0, -1), kv_spec(kcol0, 0), kv_spec(kcol0, 1),
                  kv_spec(vcol0, -1), kv_spec(vcol0, 0), kv_spec(vcol0, 1),
                  pl.BlockSpec((None,) + bias.shape[1:], lambda g, b, i: (g, 0, 0, 0, 0))],
        out_specs=pl.BlockSpec(blk, lambda g, b, i: (b * nb + i, 0, g)),
        out_shape=jax.ShapeDtypeStruct((T // GRID_W, GRID_W, D_MODEL), F32),
        scratch_shapes=[pltpu.VMEM((3 * R, GRID_W, width), BF16),
                        pltpu.VMEM((3 * R, span, width), BF16),
                        pltpu.VMEM((3 * R, GRID_W, width), BF16),
                        pltpu.VMEM((3 * R, span, width), BF16)],
        compiler_params=_cparams(("parallel", "parallel", "parallel")),
        name="natten",
    )(qkv3, qkv3, qkv3, qkv3, qkv3, qkv3, qkv3, bias)
    return out.reshape(T, D_MODEL)


def _ffn_kernel(x_ref, gate_ref, wg_ref, wu_ref, wd_ref, o_ref, wgb_ref, wub_ref, wdb_ref):
    @pl.when(pl.program_id(1) == 0)
    def _():
        wgb_ref[...] = wg_ref[...].astype(BF16)
        wub_ref[...] = wu_ref[...].astype(BF16)
        wdb_ref[...] = wd_ref[...].astype(BF16)

    x = x_ref[...].astype(BF16)
    a = jnp.dot(x, wgb_ref[...], preferred_element_type=F32)
    b = jnp.dot(x, wub_ref[...], preferred_element_type=F32)
    hidden = (a * jax.nn.sigmoid(a) * b).astype(BF16)
    o_ref[...] = jnp.dot(hidden, wdb_ref[...], preferred_element_type=F32) * gate_ref[...]


def _expert_ffn(xe, gates, w_gate, w_up, w_down, *, layer):
    E, cap, D = xe.shape
    Fh = w_gate.shape[-1]
    tm = 512
    wspec = lambda a, b: pl.BlockSpec((None, None, a, b), lambda e, i: (layer, e, 0, 0))
    return pl.pallas_call(
        _ffn_kernel,
        grid=(E, cap // tm),
        in_specs=[pl.BlockSpec((None, tm, D), lambda e, i: (e, i, 0)),
                  pl.BlockSpec((None, tm, 1), lambda e, i: (e, i, 0)),
                  wspec(D, Fh), wspec(D, Fh), wspec(Fh, D)],
        out_specs=pl.BlockSpec((None, tm, D), lambda e, i: (e, i, 0)),
        out_shape=jax.ShapeDtypeStruct((E, cap, D), F32),
        scratch_shapes=[pltpu.VMEM((D, Fh), BF16), pltpu.VMEM((D, Fh), BF16), pltpu.VMEM((Fh, D), BF16)],
        compiler_params=_cparams(("parallel", "arbitrary")),
        name="expert_ffn",
    )(xe, gates[..., None], w_gate, w_up, w_down)


def _split_dot(p, rows):
    hi = rows.astype(BF16)
    lo = (rows - hi.astype(F32)).astype(BF16)
    return jnp.dot(p, hi, preferred_element_type=F32) + jnp.dot(p, lo, preferred_element_type=F32)


def _combine_kernel(s0_ref, fast_ref, pos_ref, x_ref, g_ref, b_ref, ye_ref, o_ref,
                    stage_ref, slow_ref, f_ref, sem_ref, slow_sem, *, nblk, cap):
    i = pl.program_id(0)
    slot = i % 2
    E = N_EXPERTS
    TB = COMB_TOKENS

    def first_slot(blk, e):
        return s0_ref[e * (nblk + 1) + blk]

    def window_base(blk, e, win):
        return jnp.minimum((first_slot(blk, e) // 8) * 8, cap - win)

    def window_copy(blk, e, sl):
        src = pl.multiple_of(e * cap + window_base(blk, e, COMB_WIN), 8)
        return pltpu.make_async_copy(ye_ref.at[pl.ds(src, COMB_WIN)],
                                     stage_ref.at[sl, pl.ds(e * COMB_WIN, COMB_WIN)], sem_ref.at[sl])

    @pl.when(i == 0)
    def _():
        slow_ref[...] = jnp.zeros_like(slow_ref)
        for e in range(E):
            window_copy(0, e, 0).start()

    @pl.when(i + 1 < nblk)
    def _():
        for e in range(E):
            window_copy(i + 1, e, 1 - slot).start()

    for e in range(E):
        window_copy(i, e, slot).wait()

    pos = pos_ref[...]
    lane_e = lax.broadcasted_iota(jnp.int32, (TB, E), 1)

    @pl.when(fast_ref[i] == 1)
    def _():
        base = jnp.zeros((1, E), jnp.int32)
        for e in range(E):
            base = jnp.where(lane_e[0:1] == e, window_base(i, e, COMB_WIN), base)
        kidx = jnp.where(pos >= 0, pos - base, -1)
        lane = lax.broadcasted_iota(jnp.int32, (TB, 2 * COMB_WIN), 1)
        cols = []
        for c in range(E // 2):
            k0 = kidx[:, 2 * c:2 * c + 1]
            k1 = kidx[:, 2 * c + 1:2 * c + 2]
            k1 = jnp.where(k1 >= 0, k1 + COMB_WIN, -1)
            cols.append(jnp.where((lane == k0) | (lane == k1), 1.0, 0.0).astype(BF16))
        f_ref[...] = _split_dot(jnp.concatenate(cols, axis=1), stage_ref[slot])

    @pl.when(fast_ref[i] == 0)
    def _():
        lane = lax.broadcasted_iota(jnp.int32, (TB, COMB_SLOW_PAD), 1)

        def body(e, acc):
            base = window_base(i, e, COMB_SLOW_WIN)
            src = pl.multiple_of(e * cap + base, 8)
            cp = pltpu.make_async_copy(ye_ref.at[pl.ds(src, COMB_SLOW_WIN)],
                                       slow_ref.at[pl.ds(0, COMB_SLOW_WIN)], slow_sem)
            cp.start()
            cp.wait()
            col = jnp.max(jnp.where(lane_e == e, pos, -1), axis=1, keepdims=True)
            k = jnp.where(col >= 0, col - base, -1)
            p = jnp.where(lane == k, 1.0, 0.0).astype(BF16)
            return acc + _split_dot(p, slow_ref[...])

        f_ref[...] = lax.fori_loop(0, E, body, jnp.zeros(f_ref.shape, F32))

    o_ref[...] = _layer_norm(DN_ALPHA * x_ref[...] + f_ref[...], g_ref[...], b_ref[...])


def _combine(ye2d, pos_t, s0, fast, x2d, ln_g, ln_b, *, cap):
    T, D = x2d.shape
    E = N_EXPERTS
    TB = COMB_TOKENS
    nblk = T // TB
    assert cap >= COMB_SLOW_WIN and cap % 8 == 0 and T % TB == 0
    row = lambda i, s0, fl: (i, 0)
    const = lambda i, s0, fl: (0, 0)
    return pl.pallas_call(
        functools.partial(_combine_kernel, nblk=nblk, cap=cap),
        grid_spec=pltpu.PrefetchScalarGridSpec(
            num_scalar_prefetch=2,
            grid=(nblk,),
            in_specs=[pl.BlockSpec((TB, E), row), pl.BlockSpec((TB, D), row),
                      pl.BlockSpec((1, D), const), pl.BlockSpec((1, D), const),
                      pl.BlockSpec(memory_space=pl.ANY)],
            out_specs=pl.BlockSpec((TB, D), row),
            scratch_shapes=[pltpu.VMEM((2, E * COMB_WIN, D), F32),
                            pltpu.VMEM((COMB_SLOW_PAD, D), F32),
                            pltpu.VMEM((TB, D), F32),
                            pltpu.SemaphoreType.DMA((2,)),
                            pltpu.SemaphoreType.DMA(())]),
        out_shape=jax.ShapeDtypeStruct((T, D), F32),
        compiler_params=_cparams(("arbitrary",)),
        name="combine_ln",
    )(s0, fast, pos_t, x2d, ln_g.reshape(1, -1), ln_b.reshape(1, -1), ye2d)


def _select_kernel(aff_ref, u_ref, l_ref, pos_ref, off_ref, *, cap):
    E, R, L = aff_ref.shape
    bits = lax.bitcast_convert_type(aff_ref[...], jnp.int32)

    def count(mask):
        ones = jnp.where(mask, 1.0, 0.0)
        return jnp.sum(jnp.sum(ones, axis=2, keepdims=True), axis=1, keepdims=True)

    def step(b, prefix):
        cand = prefix | jnp.left_shift(jnp.int32(1), 30 - b)
        return jnp.where(count(bits >= cand) >= cap, cand, prefix)

    tau = lax.fori_loop(0, 31, step, jnp.zeros((E, 1, 1), jnp.int32))

    def cumsum(mask):
        x2 = jnp.where(mask, 1.0, 0.0).astype(BF16).reshape(E * R, L)
        within = jnp.dot(x2, u_ref[...], preferred_element_type=F32).reshape(E, R, L)
        tot = jnp.broadcast_to(within[:, :, L - 1:L], (E, R, L)).astype(BF16)
        before = jnp.stack([jnp.dot(l_ref[...], tot[e], preferred_element_type=F32) for e in range(E)])
        return within + before, before

    above = bits > tau
    tied = bits == tau
    need = cap - count(above)
    tied_rank, _ = cumsum(tied)
    sel = above | (tied & (tied_rank <= need))
    csum, before = cumsum(sel)
    pos_ref[...] = jnp.where(sel, csum - 1.0, -1.0).astype(jnp.int32)
    off_ref[...] = before.astype(jnp.int32)


def _compact_kernel(off_ref, pos_ref, aff_ref, acc_ref, *, rows, cap):
    n_rows, L = pos_ref.shape
    nchunk = cap // L
    acc_ref[...] = jnp.zeros_like(acc_ref)
    slot_in_chunk = lax.broadcasted_iota(jnp.int32, (L, L), 0)
    field = lax.broadcasted_iota(jnp.int32, (8, L), 0)
    lane = lax.broadcasted_iota(jnp.int32, (8, L), 1).astype(F32)

    def body(n, carry):
        e = n // rows
        r = n - e * rows
        chunk = off_ref[n] // L
        rel = pos_ref[pl.ds(n, 1), :] - chunk * L
        g = aff_ref[pl.ds(n, 1), :]
        g1 = g.astype(BF16).astype(F32)
        g2 = (g - g1).astype(BF16).astype(F32)
        g3 = g - g1 - g2
        vals = jnp.where(field == 0, lane,
                         jnp.where(field == 1, 1.0,
                                   jnp.where(field == 2, g1,
                                             jnp.where(field == 3, g2,
                                                       jnp.where(field == 4, g3, 0.0))))).astype(BF16)
        tok0 = jnp.asarray(r * L, F32)
        for half in range(2):
            hit = jnp.where(rel == slot_in_chunk + half * L, 1.0, 0.0).astype(BF16)
            res = lax.dot_general(vals, hit, (((1,), (1,)), ((), ())), preferred_element_type=F32)
            res = res + jnp.where(field == 0, tok0 * res[1:2, :], 0.0)
            acc_ref[e * nchunk + chunk + half] += res
        return carry

    lax.fori_loop(0, n_rows, body, 0, unroll=16)


def _route(aff_t, cap):
    E, T = aff_t.shape
    L = 128
    R = T // L
    assert T % L == 0 and cap % L == 0 and R % 16 == 0
    u = jnp.asarray(np.arange(L)[:, None] <= np.arange(L)[None, :], BF16)
    low = jnp.asarray(np.arange(R)[None, :] < np.arange(R)[:, None], BF16)
    aff3 = aff_t.reshape(E, R, L)
    full = lambda shape: pl.BlockSpec(shape, lambda i: (0,) * len(shape))
    pos3, before3 = pl.pallas_call(
        functools.partial(_select_kernel, cap=cap),
        grid=(1,),
        in_specs=[full((E, R, L)), full((L, L)), full((R, R))],
        out_specs=[full((E, R, L)), full((E, R, L))],
        out_shape=[jax.ShapeDtypeStruct((E, R, L), jnp.int32)] * 2,
        compiler_params=_cparams(("arbitrary",)),
        name="route_select",
    )(aff3, u, low)
    row_off = before3[:, :, 0]
    nchunk = cap // L
    acc = pl.pallas_call(
        functools.partial(_compact_kernel, rows=R, cap=cap),
        grid_spec=pltpu.PrefetchScalarGridSpec(
            num_scalar_prefetch=1,
            grid=(1,),
            in_specs=[pl.BlockSpec((E * R, L), lambda i, off: (0, 0)),
                      pl.BlockSpec((E * R, L), lambda i, off: (0, 0))],
            out_specs=pl.BlockSpec((E * nchunk + 1, 8, L), lambda i, off: (0, 0, 0))),
        out_shape=jax.ShapeDtypeStruct((E * nchunk + 1, 8, L), F32),
        compiler_params=_cparams(("arbitrary",)),
        name="route_compact",
    )(row_off.reshape(-1), pos3.reshape(E * R, L), aff3.reshape(E * R, L))
    acc = acc[:E * nchunk]
    idx = acc[:, 0, :].astype(jnp.int32).reshape(E, cap)
    gates = ((acc[:, 2, :] + acc[:, 3, :]) + acc[:, 4, :]).reshape(E, cap)
    pos_t = pos3.reshape(E, T).T
    s0 = jnp.concatenate([row_off[:, ::COMB_TOKENS // L], jnp.full((E, 1), cap, jnp.int32)], axis=1)
    fast = jnp.all(s0[:, 1:] - s0[:, :-1] <= COMB_FAST_MAX, axis=0).astype(jnp.int32)
    return idx, gates, pos_t, s0.reshape(-1), fast


def _moe_ln(x2d, aff_t, w_gate, w_up, w_down, ln_g, ln_b, *, layer):
    T, D = x2d.shape
    cap = CAPACITY_FACTOR * T // N_EXPERTS
    idx, gates, pos_t, s0, fast = _route(aff_t, cap)
    xe = x2d.at[idx].get(mode="promise_in_bounds")
    ye = _expert_ffn(xe, gates, w_gate, w_up, w_down, layer=layer)
    return _combine(ye.reshape(N_EXPERTS * cap, D), pos_t, s0, fast, x2d, ln_g, ln_b, cap=cap)


def _trunk(x, ret_w_in, ret_decay, ret_w_out, na_w_in, na_rpb, na_w_out,
           ln_mix_g, ln_mix_b, ln_ffn_g, ln_ffn_b, w_router, w_gate, w_up, w_down):
    B, S, D = x.shape
    x2d = x.reshape(B * S, D)
    cos, sin = _rotary_tables(S)
    for i in range(DEPTH):
        j = i // 2
        if i % 2 == 0:
            q, k, v, g = _proj(x2d, ret_w_in[j], cos, sin, mode="ret", seq=S)
            y = _retention_core(q, k, v, ret_decay[j], batch=B, seq=S)
            x2d, aff_t = _mix_out(y, g, x2d, ret_w_out[j], ln_mix_g[i], ln_mix_b[i], w_router[i], ret=True)
        else:
            qkv = _proj(x2d, na_w_in[j], cos, sin, mode="na", seq=S)
            o = _na_core(qkv, na_rpb[j], batch=B, seq=S)
            x2d, aff_t = _mix_out(o, None, x2d, na_w_out[j], ln_mix_g[i], ln_mix_b[i], w_router[i], ret=False)
        x2d = _moe_ln(x2d, aff_t, w_gate, w_up, w_down, ln_ffn_g[i], ln_ffn_b[i], layer=i)
    return x2d.reshape(B, S, D)


def kernel(x_prompt, x_sample, ret_w_in, ret_decay, ret_w_out, na_w_in, na_rpb, na_w_out, ln_mix_g, ln_mix_b,
           ln_ffn_g, ln_ffn_b, w_router, w_gate, w_up, w_down):
    weights = (ret_w_in, ret_decay, ret_w_out, na_w_in, na_rpb, na_w_out, ln_mix_g, ln_mix_b,
               ln_ffn_g, ln_ffn_b, w_router, w_gate, w_up, w_down)
    return _trunk(x_prompt, *weights), _trunk(x_sample, *weights)
```

```python
import functools

import jax
import jax.numpy as jnp
import numpy as np
from jax import lax
from jax.experimental import pallas as pl
from jax.experimental.pallas import tpu as pltpu

F32 = jnp.float32
BF16 = jnp.bfloat16

D_MODEL = 1024
DEPTH = 2
GRID_W = 64
RET_HEADS = 4
RET_DK = 256
RET_DV = 512
RET_QK_W = RET_HEADS * RET_DK
RET_V_W = RET_HEADS * RET_DV
ROPE_BASE = 10000.0
NA_HEADS = 32
NA_HD = D_MODEL // NA_HEADS
NA_KH = 8
NA_KW = 16
N_EXPERTS = 16
EXPERT_FF = 1024
CAPACITY_FACTOR = 2
DN_ALPHA = (2 * DEPTH) ** 0.25
LN_EPS = 1e-5
GN_EPS = 1e-6

VMEM_LIMIT_BYTES = 56 * 1024 * 1024
PROJ_COL_CHUNK = 1024
RET_CHUNK = 256
NA_ROWS_PER_STEP = 8
NA_HEAD_GROUP = 8
NA_QGROUP = 16
NA_WIN_COLS = 32
NA_WIN_STARTS = (0, 8, 24, 32)
NEG_BIG = -1e30
COMB_TOKENS = 256
COMB_WIN = 64
COMB_FAST_MAX = COMB_WIN - 8
COMB_SLOW_WIN = COMB_TOKENS + 8
COMB_SLOW_PAD = 384


def _cparams(sem):
    return pltpu.CompilerParams(dimension_semantics=sem, vmem_limit_bytes=VMEM_LIMIT_BYTES)


def _proj_kernel(x_ref, w_ref, cos_ref, sin_ref, *o_refs, mode):
    xb = x_ref[...].astype(BF16)
    tn = PROJ_COL_CHUNK

    def chunk(c):
        return jnp.dot(xb, w_ref[:, c * tn:(c + 1) * tn], preferred_element_type=F32)

    if mode == "na":
        (o_ref,) = o_refs
        for c in range(w_ref.shape[1] // tn):
            acc = chunk(c)
            o_ref[:, c * tn:(c + 1) * tn] = acc * (NA_HD ** -0.5) if c == 0 else acc
    else:
        q_ref, k_ref, v_ref, g_ref = o_refs
        cs = cos_ref[...]
        sn = sin_ref[...]
        half = RET_DK // 2
        for c, dst, scale in ((0, q_ref, 1.0), (1, k_ref, RET_DK ** -0.5)):
            acc = chunk(c)
            for h in range(RET_HEADS):
                x1 = acc[:, h * RET_DK:h * RET_DK + half] * scale
                x2 = acc[:, h * RET_DK + half:(h + 1) * RET_DK] * scale
                dst[:, h * RET_DK:h * RET_DK + half] = (x1 * cs - x2 * sn).astype(dst.dtype)
                dst[:, h * RET_DK + half:(h + 1) * RET_DK] = (x1 * sn + x2 * cs).astype(dst.dtype)
        for c in range(2):
            v_ref[:, c * tn:(c + 1) * tn] = chunk(2 + c).astype(v_ref.dtype)
            g_ref[:, c * tn:(c + 1) * tn] = chunk(4 + c)


def _proj(x2d, w, cos, sin, *, mode, seq):
    T, K = x2d.shape
    N = w.shape[1]
    tm = 256
    nseq = seq // tm
    row = lambda i: (i, 0)
    if mode == "na":
        widths, dtypes = (N,), (F32,)
    else:
        assert N == 2 * RET_QK_W + 2 * RET_V_W and RET_QK_W == PROJ_COL_CHUNK and RET_V_W == 2 * PROJ_COL_CHUNK
        widths, dtypes = (RET_QK_W, RET_QK_W, RET_V_W, RET_V_W), (BF16, F32, BF16, F32)
    outs = pl.pallas_call(
        functools.partial(_proj_kernel, mode=mode),
        grid=(T // tm,),
        in_specs=[
            pl.BlockSpec((tm, K), row),
            pl.BlockSpec((K, N), lambda i: (0, 0)),
            pl.BlockSpec((tm, RET_DK // 2), lambda i: (i % nseq, 0)),
            pl.BlockSpec((tm, RET_DK // 2), lambda i: (i % nseq, 0)),
        ],
        out_specs=[pl.BlockSpec((tm, n), row) for n in widths],
        out_shape=[jax.ShapeDtypeStruct((T, n), dt) for n, dt in zip(widths, dtypes)],
        compiler_params=_cparams(("parallel",)),
        name="proj_" + mode,
    )(x2d, w.astype(BF16), cos, sin)
    return outs[0] if mode == "na" else outs


def _rotary_tables(seq):
    inv = 1.0 / (ROPE_BASE ** (jnp.arange(0, RET_DK, 2, dtype=F32) / RET_DK))
    ang = jnp.arange(seq, dtype=F32)[:, None] * inv[None, :]
    return jnp.cos(ang), jnp.sin(ang)


def _ret_kernel(cdec_ref, q_ref, k_ref, v_ref, dm_ref, cd_ref, sd_ref, o_ref, state_ref):
    d = pl.program_id(0)
    c = pl.program_id(2)

    @pl.when(c == 0)
    def _():
        state_ref[...] = jnp.zeros_like(state_ref)

    for h in range(RET_HEADS):
        q = q_ref[:, h * RET_DK:(h + 1) * RET_DK].astype(BF16)
        k = k_ref[:, h * RET_DK:(h + 1) * RET_DK]
        v = v_ref[:, h * RET_DV:(h + 1) * RET_DV].astype(BF16)
        scores = lax.dot_general(q, k.astype(BF16), (((1,), (1,)), ((), ())),
                                 preferred_element_type=F32) * dm_ref[h]
        o = jnp.dot(scores.astype(BF16), v, preferred_element_type=F32)
        state = state_ref[h]
        o = o + jnp.dot(q, state.astype(BF16), preferred_element_type=F32) * cd_ref[h]
        ks = (k * sd_ref[h]).astype(BF16)
        state_ref[h] = state * cdec_ref[d * RET_HEADS + h] + lax.dot_general(
            ks, v, (((0,), (0,)), ((), ())), preferred_element_type=F32)
        o_ref[:, h * RET_DV:(h + 1) * RET_DV] = o


def _ret_tables(decay_logit, C):
    log_g = jax.nn.log_sigmoid(decay_logit.astype(F32))
    i = jnp.arange(C, dtype=F32)
    diff = i[:, None] - i[None, :]
    lf = log_g[0][:, None, None]
    lb = log_g[1][:, None, None]
    dm_f = jnp.where((diff >= 0)[None], jnp.exp(lf * jnp.maximum(diff, 0.0)[None]), 0.0)
    dm_b = jnp.where((diff < 0)[None], jnp.exp(lb * jnp.maximum(-diff, 0.0)[None]), 0.0)
    cd_f = jnp.exp(log_g[0][:, None] * (i[None, :] + 1.0))
    cd_b = jnp.exp(log_g[1][:, None] * (C - i[None, :]))
    sd_f = jnp.exp(log_g[0][:, None] * (C - 1.0 - i[None, :]))
    sd_b = jnp.exp(log_g[1][:, None] * i[None, :])
    dm = jnp.stack([dm_f, dm_b]).astype(F32)
    cd = jnp.stack([cd_f, cd_b]).astype(F32)[..., None]
    sd = jnp.stack([sd_f, sd_b]).astype(F32)[..., None]
    cdec = jnp.exp(log_g * C).reshape(-1).astype(F32)
    return dm, cd, sd, cdec


def _retention_core(q, k, v, decay_logit, *, batch, seq):
    T = q.shape[0]
    C = RET_CHUNK
    NC = seq // C
    H = RET_HEADS
    dm, cd, sd, cdec = _ret_tables(decay_logit, C)

    def row(d, b, c):
        return b * NC + c + d * (NC - 1 - 2 * c)

    return pl.pallas_call(
        _ret_kernel,
        grid=(2, batch, NC),
        in_specs=[
            pl.BlockSpec(memory_space=pltpu.SMEM),
            pl.BlockSpec((C, RET_QK_W), lambda d, b, c: (row(d, b, c), 0)),
            pl.BlockSpec((C, RET_QK_W), lambda d, b, c: (row(d, b, c), 0)),
            pl.BlockSpec((C, RET_V_W), lambda d, b, c: (row(d, b, c), 0)),
            pl.BlockSpec((None, H, C, C), lambda d, b, c: (d, 0, 0, 0)),
            pl.BlockSpec((None, H, C, 1), lambda d, b, c: (d, 0, 0, 0)),
            pl.BlockSpec((None, H, C, 1), lambda d, b, c: (d, 0, 0, 0)),
        ],
        out_specs=pl.BlockSpec((None, C, RET_V_W), lambda d, b, c: (d, row(d, b, c), 0)),
        out_shape=jax.ShapeDtypeStruct((2, T, RET_V_W), F32),
        scratch_shapes=[pltpu.VMEM((H, RET_DK, RET_DV), F32)],
        compiler_params=_cparams(("parallel", "parallel", "arbitrary")),
        name="retention",
    )(cdec, q, k, v, dm, cd, sd)


def _layer_norm(u, g, b):
    mu = jnp.mean(u, axis=-1, keepdims=True)
    var = jnp.mean(jnp.square(u - mu), axis=-1, keepdims=True)
    return (u - mu) * lax.rsqrt(var + LN_EPS) * g + b


def _mix_out_kernel(*refs, ret):
    if ret:
        y_ref, g_ref, x_ref, w_ref, lng_ref, lnb_ref, wr_ref, o_ref, ob_ref, aff_ref, wbf_ref = refs
    else:
        y_ref, x_ref, w_ref, lng_ref, lnb_ref, wr_ref, o_ref, ob_ref, aff_ref, wbf_ref = refs

    @pl.when(pl.program_id(0) == 0)
    def _():
        wbf_ref[...] = w_ref[...].astype(BF16)

    if ret:
        y = y_ref[0] + y_ref[1]
        parts = []
        for h in range(RET_HEADS):
            yh = y[:, h * RET_DV:(h + 1) * RET_DV]
            mu = jnp.mean(yh, axis=-1, keepdims=True)
            var = jnp.mean(jnp.square(yh - mu), axis=-1, keepdims=True)
            yn = (yh - mu) * lax.rsqrt(var + GN_EPS)
            gh = g_ref[:, h * RET_DV:(h + 1) * RET_DV]
            parts.append((gh * jax.nn.sigmoid(gh) * yn).astype(BF16))
        z = jnp.concatenate(parts, axis=1)
    else:
        z = y_ref[...].astype(BF16)
    mixed = jnp.dot(z, wbf_ref[...], preferred_element_type=F32)
    xn = _layer_norm(DN_ALPHA * x_ref[...] + mixed, lng_ref[...], lnb_ref[...])
    o_ref[...] = xn
    xb = xn.astype(BF16)
    ob_ref[...] = xb
    logits = lax.dot_general(wr_ref[...].astype(BF16), xb, (((1,), (1,)), ((), ())),
                             preferred_element_type=F32)
    e = jnp.exp(logits - jnp.max(logits, axis=0, keepdims=True))
    aff_ref[...] = e / jnp.sum(e, axis=0, keepdims=True)


def _mix_out(y, gsrc, x2d, w_out, ln_g, ln_b, w_router, *, ret):
    T = x2d.shape[0]
    K = w_out.shape[0]
    tm = 256
    wr_t = w_router.T
    row = lambda i: (i, 0)
    const = lambda i: (0, 0)
    if ret:
        in_specs = [pl.BlockSpec((2, tm, K), lambda i: (0, i, 0)),
                    pl.BlockSpec((tm, K), row)]
        args = [y, gsrc]
    else:
        in_specs = [pl.BlockSpec((tm, K), row)]
        args = [y]
    in_specs += [pl.BlockSpec((tm, D_MODEL), row),
                 pl.BlockSpec((K, D_MODEL), const),
                 pl.BlockSpec((1, D_MODEL), const),
                 pl.BlockSpec((1, D_MODEL), const),
                 pl.BlockSpec((N_EXPERTS, D_MODEL), const)]
    args += [x2d, w_out, ln_g.reshape(1, -1), ln_b.reshape(1, -1), wr_t]
    return pl.pallas_call(
        functools.partial(_mix_out_kernel, ret=ret),
        grid=(T // tm,),
        in_specs=in_specs,
        out_specs=[pl.BlockSpec((tm, D_MODEL), row),
                   pl.BlockSpec((tm, D_MODEL), row),
                   pl.BlockSpec((N_EXPERTS, tm), lambda i: (0, i))],
        out_shape=[jax.ShapeDtypeStruct((T, D_MODEL), F32),
                   jax.ShapeDtypeStruct((T, D_MODEL), BF16),
                   jax.ShapeDtypeStruct((N_EXPERTS, T), F32)],
        scratch_shapes=[pltpu.VMEM((K, D_MODEL), BF16)],
        compiler_params=_cparams(("arbitrary",)),
        name="mix_out_ret" if ret else "mix_out_na",
    )(*args)


def _na_kernel(q_ref, k0_ref, k1_ref, k2_ref, v0_ref, v1_ref, v2_ref, bias_ref, o_ref,
               ka_ref, kb_ref, va_ref, vb_ref, *, rows):
    i = pl.program_id(2)
    R = NA_ROWS_PER_STEP
    lo, hi = NA_WIN_STARTS[1], NA_WIN_STARTS[2] + NA_WIN_COLS
    for t, (kr, vr) in enumerate(((k0_ref, v0_ref), (k1_ref, v1_ref), (k2_ref, v2_ref))):
        ka_ref[t * R:(t + 1) * R] = kr[...].astype(BF16)
        kb_ref[t * R:(t + 1) * R] = kr[:, lo:hi, :].astype(BF16)
        va_ref[t * R:(t + 1) * R] = vr[...].astype(BF16)
        vb_ref[t * R:(t + 1) * R] = vr[:, lo:hi, :].astype(BF16)

    nrow = NA_HEAD_GROUP * NA_QGROUP
    width = NA_HEAD_GROUP * NA_HD
    rid = lax.broadcasted_iota(jnp.int32, (nrow, width), 0) // NA_QGROUP
    cid = lax.broadcasted_iota(jnp.int32, (nrow, width), 1) // NA_HD
    head_mask = (rid == cid).astype(F32)
    nkeys = NA_KH * NA_WIN_COLS

    def body(rr, carry):
        r = i * R + rr
        rs = jnp.clip(r - NA_KH // 2, 0, rows - NA_KH)
        delta = r - rs
        rl = rs - (i * R - R)
        def window(a_ref, b_ref, j):
            c0 = NA_WIN_STARTS[j]
            if j in (0, 3):
                win = a_ref[pl.ds(rl, NA_KH), c0:c0 + NA_WIN_COLS, :]
            else:
                win = b_ref[pl.ds(rl, NA_KH), c0 - lo:c0 - lo + NA_WIN_COLS, :]
            return win.reshape(nkeys, width)

        scores = []
        for j in range(len(NA_WIN_STARTS)):
            qg = q_ref[rr, j * NA_QGROUP:(j + 1) * NA_QGROUP, :]
            lhs = (jnp.tile(qg, (NA_HEAD_GROUP, 1)) * head_mask).astype(BF16)
            scores.append(lax.dot_general(lhs, window(ka_ref, kb_ref, j), (((1,), (1,)), ((), ())),
                                          preferred_element_type=F32))
        s = jnp.concatenate(scores, axis=0) + bias_ref[delta].reshape(len(scores) * nrow, nkeys)
        e = jnp.exp(s - jnp.max(s, axis=-1, keepdims=True))
        p = (e * (1.0 / jnp.sum(e, axis=-1, keepdims=True))).astype(BF16)
        for j in range(len(NA_WIN_STARTS)):
            pv = jnp.dot(p[j * nrow:(j + 1) * nrow], window(va_ref, vb_ref, j),
                         preferred_element_type=F32) * head_mask
            out = pv[0:NA_QGROUP]
            for g in range(1, NA_HEAD_GROUP):
                out = out + pv[g * NA_QGROUP:(g + 1) * NA_QGROUP]
            o_ref[rr, j * NA_QGROUP:(j + 1) * NA_QGROUP, :] = out.astype(o_ref.dtype)
        return carry

    lax.fori_loop(0, R, body, 0, unroll=True)


def _na_bias_table(rpb):
    HG = NA_HEADS // NA_HEAD_GROUP
    nj = GRID_W // NA_QGROUP
    delta = np.arange(NA_KH)[:, None]
    a = np.arange(NA_KH)[None, :]
    row_sel = np.eye(2 * NA_KH - 1, dtype=np.float32)[a - delta + (NA_KH - 1)]
    w = (np.arange(nj)[:, None] * NA_QGROUP + np.arange(NA_QGROUP)[None, :])[:, :, None]
    col = np.asarray(NA_WIN_STARTS)[:, None, None] + np.arange(NA_WIN_COLS)[None, None, :]
    cs = np.clip(w - NA_KW // 2, 0, GRID_W - NA_KW)
    valid = (col >= cs) & (col < cs + NA_KW)
    col_off = np.clip(col - w + (NA_KW - 1), 0, 2 * NA_KW - 2)
    col_sel = np.eye(2 * NA_KW - 1, dtype=np.float32)[col_off] * valid[..., None]
    rpb_g = rpb.astype(F32).reshape(HG, NA_HEAD_GROUP, 2 * NA_KH - 1, 2 * NA_KW - 1)
    vals = jnp.einsum("ghrc,dar,jqkc->gdjhqak", rpb_g, row_sel, col_sel, precision=lax.Precision.HIGHEST)
    vals = jnp.where(valid[None, None, :, None, :, None, :], vals, NEG_BIG)
    return vals.reshape(HG, NA_KH, nj, NA_HEAD_GROUP * NA_QGROUP, NA_KH * NA_WIN_COLS)


def _na_core(qkv, rpb, *, batch, seq):
    T = qkv.shape[0]
    rows = seq // GRID_W
    R = NA_ROWS_PER_STEP
    assert rows % R == 0 and rows >= NA_KH
    nb = rows // R
    HG = NA_HEADS // NA_HEAD_GROUP
    width = NA_HEAD_GROUP * NA_HD
    qkv3 = qkv.reshape(T // GRID_W, GRID_W, 3 * D_MODEL)
    bias = _na_bias_table(rpb)
    kcol0 = D_MODEL // width
    vcol0 = 2 * D_MODEL // width
    blk = (R, GRID_W, width)

    def kv_spec(col0, off):
        return pl.BlockSpec(blk, lambda g, b, i: (b * nb + jnp.clip(i + off, 0, nb - 1), 0, col0 + g))

    span = NA_WIN_STARTS[2] + NA_WIN_COLS - NA_WIN_STARTS[1]
    out = pl.pallas_call(
        functools.partial(_na_kernel, rows=rows),
        grid=(HG, batch, nb),
        in_specs=[pl.BlockSpec(blk, lambda g, b, i: (b * nb + i, 0, g)),
                  kv_spec(kcol0, -1), kv_spec(kcol0, 0), kv_spec(kcol0, 1),
                  kv_spec(vcol0, -1), kv_spec(vcol0, 0), kv_spec(vcol0, 1),
                  pl.BlockSpec((None,) + bias.shape[1:], lambda g, b, i: (g, 0, 0, 0, 0))],
        out_specs=pl.BlockSpec(blk, lambda g, b, i: (b * nb + i, 0, g)),
        out_shape=jax.ShapeDtypeStruct((T // GRID_W, GRID_W, D_MODEL), BF16),
        scratch_shapes=[pltpu.VMEM((3 * R, GRID_W, width), BF16),
                        pltpu.VMEM((3 * R, span, width), BF16),
                        pltpu.VMEM((3 * R, GRID_W, width), BF16),
                        pltpu.VMEM((3 * R, span, width), BF16)],
        compiler_params=_cparams(("parallel", "parallel", "parallel")),
        name="natten",
    )(qkv3, qkv3, qkv3, qkv3, qkv3, qkv3, qkv3, bias)
    return out.reshape(T, D_MODEL)


def _ffn_kernel(x_ref, gate_ref, wg_ref, wu_ref, wd_ref, o_ref, wgb_ref, wub_ref, wdb_ref):
    @pl.when(pl.program_id(1) == 0)
    def _():
        wgb_ref[...] = wg_ref[...].astype(BF16)
        wub_ref[...] = wu_ref[...].astype(BF16)
        wdb_ref[...] = wd_ref[...].astype(BF16)

    x = x_ref[...].astype(BF16)
    a = jnp.dot(x, wgb_ref[...], preferred_element_type=F32)
    b = jnp.dot(x, wub_ref[...], preferred_element_type=F32)
    hidden = (a * jax.nn.sigmoid(a) * b).astype(BF16)
    o_ref[...] = jnp.dot(hidden, wdb_ref[...], preferred_element_type=F32) * gate_ref[...]


def _expert_ffn(xe, gates, w_gate, w_up, w_down, *, layer):
    E, cap, D = xe.shape
    Fh = w_gate.shape[-1]
    tm = 512
    wspec = lambda a, b: pl.BlockSpec((None, None, a, b), lambda e, i: (layer, e, 0, 0))
    return pl.pallas_call(
        _ffn_kernel,
        grid=(E, cap // tm),
        in_specs=[pl.BlockSpec((None, tm, D), lambda e, i: (e, i, 0)),
                  pl.BlockSpec((None, tm, 1), lambda e, i: (e, i, 0)),
                  wspec(D, Fh), wspec(D, Fh), wspec(Fh, D)],
        out_specs=pl.BlockSpec((None, tm, D), lambda e, i: (e, i, 0)),
        out_shape=jax.ShapeDtypeStruct((E, cap, D), F32),
        scratch_shapes=[pltpu.VMEM((D, Fh), BF16), pltpu.VMEM((D, Fh), BF16), pltpu.VMEM((Fh, D), BF16)],
        compiler_params=_cparams(("parallel", "arbitrary")),
        name="expert_ffn",
    )(xe, gates[..., None], w_gate, w_up, w_down)


def _split_dot(p, rows):
    hi = rows.astype(BF16)
    lo = (rows - hi.astype(F32)).astype(BF16)
    return jnp.dot(p, hi, preferred_element_type=F32) + jnp.dot(p, lo, preferred_element_type=F32)


def _combine_kernel(s0_ref, fast_ref, pos_ref, x_ref, g_ref, b_ref, ye_ref, o_ref,
                    stage_ref, slow_ref, f_ref, sem_ref, slow_sem, *, nblk, cap):
    i = pl.program_id(0)
    slot = i % 2
    E = N_EXPERTS
    TB = COMB_TOKENS

    def first_slot(blk, e):
        return s0_ref[e * (nblk + 1) + blk]

    def window_base(blk, e, win):
        return jnp.minimum((first_slot(blk, e) // 8) * 8, cap - win)

    def window_copy(blk, e, sl):
        src = pl.multiple_of(e * cap + window_base(blk, e, COMB_WIN), 8)
        return pltpu.make_async_copy(ye_ref.at[pl.ds(src, COMB_WIN)],
                                     stage_ref.at[sl, pl.ds(e * COMB_WIN, COMB_WIN)], sem_ref.at[sl])

    @pl.when(i == 0)
    def _():
        slow_ref[...] = jnp.zeros_like(slow_ref)
        for e in range(E):
            window_copy(0, e, 0).start()

    @pl.when(i + 1 < nblk)
    def _():
        for e in range(E):
            window_copy(i + 1, e, 1 - slot).start()

    for e in range(E):
        window_copy(i, e, slot).wait()

    pos = pos_ref[...]
    lane_e = lax.broadcasted_iota(jnp.int32, (TB, E), 1)

    @pl.when(fast_ref[i] == 1)
    def _():
        base = jnp.zeros((1, E), jnp.int32)
        for e in range(E):
            base = jnp.where(lane_e[0:1] == e, window_base(i, e, COMB_WIN), base)
        kidx = jnp.where(pos >= 0, pos - base, -1)
        lane = lax.broadcasted_iota(jnp.int32, (TB, 2 * COMB_WIN), 1)
        cols = []
        for c in range(E // 2):
            k0 = kidx[:, 2 * c:2 * c + 1]
            k1 = kidx[:, 2 * c + 1:2 * c + 2]
            k1 = jnp.where(k1 >= 0, k1 + COMB_WIN, -1)
            cols.append(jnp.where((lane == k0) | (lane == k1), 1.0, 0.0).astype(BF16))
        f_ref[...] = _split_dot(jnp.concatenate(cols, axis=1), stage_ref[slot])

    @pl.when(fast_ref[i] == 0)
    def _():
        lane = lax.broadcasted_iota(jnp.int32, (TB, COMB_SLOW_PAD), 1)

        def body(e, acc):
            base = window_base(i, e, COMB_SLOW_WIN)
            src = pl.multiple_of(e * cap + base, 8)
            cp = pltpu.make_async_copy(ye_ref.at[pl.ds(src, COMB_SLOW_WIN)],
                                       slow_ref.at[pl.ds(0, COMB_SLOW_WIN)], slow_sem)
            cp.start()
            cp.wait()
            col = jnp.max(jnp.where(lane_e == e, pos, -1), axis=1, keepdims=True)
            k = jnp.where(col >= 0, col - base, -1)
            p = jnp.where(lane == k, 1.0, 0.0).astype(BF16)
            return acc + _split_dot(p, slow_ref[...])

        f_ref[...] = lax.fori_loop(0, E, body, jnp.zeros(f_ref.shape, F32))

    o_ref[...] = _layer_norm(DN_ALPHA * x_ref[...] + f_ref[...], g_ref[...], b_ref[...])


def _combine(ye2d, pos_t, s0, fast, x2d, ln_g, ln_b, *, cap):
    T, D = x2d.shape
    E = N_EXPERTS
    TB = COMB_TOKENS
    nblk = T // TB
    assert cap >= COMB_SLOW_WIN and cap % 8 == 0 and T % TB == 0
    row = lambda i, s0, fl: (i, 0)
    const = lambda i, s0, fl: (0, 0)
    return pl.pallas_call(
        functools.partial(_combine_kernel, nblk=nblk, cap=cap),
        grid_spec=pltpu.PrefetchScalarGridSpec(
            num_scalar_prefetch=2,
            grid=(nblk,),
            in_specs=[pl.BlockSpec((TB, E), row), pl.BlockSpec((TB, D), row),
                      pl.BlockSpec((1, D), const), pl.BlockSpec((1, D), const),
                      pl.BlockSpec(memory_space=pl.ANY)],
            out_specs=pl.BlockSpec((TB, D), row),
            scratch_shapes=[pltpu.VMEM((2, E * COMB_WIN, D), F32),
                            pltpu.VMEM((COMB_SLOW_PAD, D), F32),
                            pltpu.VMEM((TB, D), F32),
                            pltpu.SemaphoreType.DMA((2,)),
                            pltpu.SemaphoreType.DMA(())]),
        out_shape=jax.ShapeDtypeStruct((T, D), F32),
        compiler_params=_cparams(("arbitrary",)),
        name="combine_ln",
    )(s0, fast, pos_t, x2d, ln_g.reshape(1, -1), ln_b.reshape(1, -1), ye2d)


def _select_kernel(aff_ref, u_ref, l_ref, pos_ref, off_ref, *, cap):
    E, R, L = aff_ref.shape
    bits = lax.bitcast_convert_type(aff_ref[...], jnp.int32)

    def count(mask):
        ones = jnp.where(mask, 1.0, 0.0)
        return jnp.sum(jnp.sum(ones, axis=2, keepdims=True), axis=1, keepdims=True)

    def step(b, prefix):
        cand = prefix | jnp.left_shift(jnp.int32(1), 30 - b)
        return jnp.where(count(bits >= cand) >= cap, cand, prefix)

    tau = lax.fori_loop(0, 31, step, jnp.zeros((E, 1, 1), jnp.int32))

    def cumsum(mask):
        x2 = jnp.where(mask, 1.0, 0.0).astype(BF16).reshape(E * R, L)
        within = jnp.dot(x2, u_ref[...], preferred_element_type=F32).reshape(E, R, L)
        tot = jnp.broadcast_to(within[:, :, L - 1:L], (E, R, L)).astype(BF16)
        before = jnp.stack([jnp.dot(l_ref[...], tot[e], preferred_element_type=F32) for e in range(E)])
        return within + before, before

    above = bits > tau
    tied = bits == tau
    need = cap - count(above)
    tied_rank, _ = cumsum(tied)
    sel = above | (tied & (tied_rank <= need))
    csum, before = cumsum(sel)
    pos_ref[...] = jnp.where(sel, csum - 1.0, -1.0).astype(jnp.int32)
    off_ref[...] = before.astype(jnp.int32)


def _compact_kernel(off_ref, pos_ref, aff_ref, acc_ref, *, rows, cap):
    n_rows, L = pos_ref.shape
    nchunk = cap // L
    acc_ref[...] = jnp.zeros_like(acc_ref)
    slot_in_chunk = lax.broadcasted_iota(jnp.int32, (L, L), 0)
    field = lax.broadcasted_iota(jnp.int32, (8, L), 0)
    lane = lax.broadcasted_iota(jnp.int32, (8, L), 1).astype(F32)

    def body(n, carry):
        e = n // rows
        r = n - e * rows
        chunk = off_ref[n] // L
        rel = pos_ref[pl.ds(n, 1), :] - chunk * L
        g = aff_ref[pl.ds(n, 1), :]
        g1 = g.astype(BF16).astype(F32)
        g2 = (g - g1).astype(BF16).astype(F32)
        g3 = g - g1 - g2
        vals = jnp.where(field == 0, lane,
                         jnp.where(field == 1, 1.0,
                                   jnp.where(field == 2, g1,
                                             jnp.where(field == 3, g2,
                                                       jnp.where(field == 4, g3, 0.0))))).astype(BF16)
        tok0 = jnp.asarray(r * L, F32)
        for half in range(2):
            hit = jnp.where(rel == slot_in_chunk + half * L, 1.0, 0.0).astype(BF16)
            res = lax.dot_general(vals, hit, (((1,), (1,)), ((), ())), preferred_element_type=F32)
            res = res + jnp.where(field == 0, tok0 * res[1:2, :], 0.0)
            acc_ref[e * nchunk + chunk + half] += res
        return carry

    lax.fori_loop(0, n_rows, body, 0, unroll=16)


def _route(aff_t, cap):
    E, T = aff_t.shape
    L = 128
    R = T // L
    assert T % L == 0 and cap % L == 0 and R % 16 == 0
    u = jnp.asarray(np.arange(L)[:, None] <= np.arange(L)[None, :], BF16)
    low = jnp.asarray(np.arange(R)[None, :] < np.arange(R)[:, None], BF16)
    aff3 = aff_t.reshape(E, R, L)
    full = lambda shape: pl.BlockSpec(shape, lambda i: (0,) * len(shape))
    pos3, before3 = pl.pallas_call(
        functools.partial(_select_kernel, cap=cap),
        grid=(1,),
        in_specs=[full((E, R, L)), full((L, L)), full((R, R))],
        out_specs=[full((E, R, L)), full((E, R, L))],
        out_shape=[jax.ShapeDtypeStruct((E, R, L), jnp.int32)] * 2,
        compiler_params=_cparams(("arbitrary",)),
        name="route_select",
    )(aff3, u, low)
    row_off = before3[:, :, 0]
    nchunk = cap // L
    acc = pl.pallas_call(
        functools.partial(_compact_kernel, rows=R, cap=cap),
        grid_spec=pltpu.PrefetchScalarGridSpec(
            num_scalar_prefetch=1,
            grid=(1,),
            in_specs=[pl.BlockSpec((E * R, L), lambda i, off: (0, 0)),
                      pl.BlockSpec((E * R, L), lambda i, off: (0, 0))],
            out_specs=pl.BlockSpec((E * nchunk + 1, 8, L), lambda i, off: (0, 0, 0))),
        out_shape=jax.ShapeDtypeStruct((E * nchunk + 1, 8, L), F32),
        compiler_params=_cparams(("arbitrary",)),
        name="route_compact",
    )(row_off.reshape(-1), pos3.reshape(E * R, L), aff3.reshape(E * R, L))
    acc = acc[:E * nchunk]
    idx = acc[:, 0, :].astype(jnp.int32).reshape(E, cap)
    gates = ((acc[:, 2, :] + acc[:, 3, :]) + acc[:, 4, :]).reshape(E, cap)
    pos_t = pos3.reshape(E, T).T
    s0 = jnp.concatenate([row_off[:, ::COMB_TOKENS // L], jnp.full((E, 1), cap, jnp.int32)], axis=1)
    fast = jnp.all(s0[:, 1:] - s0[:, :-1] <= COMB_FAST_MAX, axis=0).astype(jnp.int32)
    return idx, gates, pos_t, s0.reshape(-1), fast


def _moe_ln(x2d, xb, aff_t, w_gate, w_up, w_down, ln_g, ln_b, *, layer):
    T, D = x2d.shape
    cap = CAPACITY_FACTOR * T // N_EXPERTS
    idx, gates, pos_t, s0, fast = _route(aff_t, cap)
    xe = xb.at[idx].get(mode="promise_in_bounds")
    ye = _expert_ffn(xe, gates, w_gate, w_up, w_down, layer=layer)
    return _combine(ye.reshape(N_EXPERTS * cap, D), pos_t, s0, fast, x2d, ln_g, ln_b, cap=cap)


def _trunk(x, ret_w_in, ret_decay, ret_w_out, na_w_in, na_rpb, na_w_out,
           ln_mix_g, ln_mix_b, ln_ffn_g, ln_ffn_b, w_router, w_gate, w_up, w_down):
    B, S, D = x.shape
    x2d = x.reshape(B * S, D)
    cos, sin = _rotary_tables(S)
    for i in range(DEPTH):
        j = i // 2
        if i % 2 == 0:
            q, k, v, g = _proj(x2d, ret_w_in[j], cos, sin, mode="ret", seq=S)
            y = _retention_core(q, k, v, ret_decay[j], batch=B, seq=S)
            x2d, xb, aff_t = _mix_out(y, g, x2d, ret_w_out[j], ln_mix_g[i], ln_mix_b[i], w_router[i], ret=True)
        else:
            qkv = _proj(x2d, na_w_in[j], cos, sin, mode="na", seq=S)
            o = _na_core(qkv, na_rpb[j], batch=B, seq=S)
            x2d, xb, aff_t = _mix_out(o, None, x2d, na_w_out[j], ln_mix_g[i], ln_mix_b[i], w_router[i], ret=False)
        x2d = _moe_ln(x2d, xb, aff_t, w_gate, w_up, w_down, ln_ffn_g[i], ln_ffn_b[i], layer=i)
    return x2d.reshape(B, S, D)


def kernel(x_prompt, x_sample, ret_w_in, ret_decay, ret_w_out, na_w_in, na_rpb, na_w_out, ln_mix_g, ln_mix_b,
           ln_ffn_g, ln_ffn_b, w_router, w_gate, w_up, w_down):
    weights = (ret_w_in, ret_decay, ret_w_out, na_w_in, na_rpb, na_w_out, ln_mix_g, ln_mix_b,
               ln_ffn_g, ln_ffn_b, w_router, w_gate, w_up, w_down)
    return _trunk(x_prompt, *weights), _trunk(x_sample, *weights)
```

```python
import functools

import jax
import jax.numpy as jnp
import numpy as np
from jax import lax
from jax.experimental import pallas as pl
from jax.experimental.pallas import tpu as pltpu

F32 = jnp.float32
BF16 = jnp.bfloat16

D_MODEL = 1024
DEPTH = 2
GRID_W = 64
RET_HEADS = 4
RET_DK = 256
RET_DV = 512
RET_QK_W = RET_HEADS * RET_DK
RET_V_W = RET_HEADS * RET_DV
ROPE_BASE = 10000.0
NA_HEADS = 32
NA_HD = D_MODEL // NA_HEADS
NA_KH = 8
NA_KW = 16
N_EXPERTS = 16
EXPERT_FF = 1024
CAPACITY_FACTOR = 2
DN_ALPHA = (2 * DEPTH) ** 0.25
LN_EPS = 1e-5
GN_EPS = 1e-6

VMEM_LIMIT_BYTES = 56 * 1024 * 1024
PROJ_COL_CHUNK = 1024
RET_CHUNK = 256
NA_ROWS_PER_STEP = 8
NA_HEAD_GROUP = 8
NA_QGROUP = 16
NA_WIN_COLS = 32
NA_WIN_STARTS = (0, 8, 24, 32)
NEG_BIG = -1e30
COMB_TOKENS = 256
COMB_WIN = 64
COMB_FAST_MAX = COMB_WIN - 8
COMB_SLOW_WIN = COMB_TOKENS + 8
COMB_SLOW_PAD = 384


def _cparams(sem):
    return pltpu.CompilerParams(dimension_semantics=sem, vmem_limit_bytes=VMEM_LIMIT_BYTES)


def _proj_kernel(x_ref, w_ref, cos_ref, sin_ref, *o_refs, mode):
    xb = x_ref[...].astype(BF16)
    tn = PROJ_COL_CHUNK

    def chunk(c):
        return jnp.dot(xb, w_ref[:, c * tn:(c + 1) * tn], preferred_element_type=F32)

    if mode == "na":
        (o_ref,) = o_refs
        for c in range(w_ref.shape[1] // tn):
            acc = chunk(c)
            o_ref[:, c * tn:(c + 1) * tn] = acc * (NA_HD ** -0.5) if c == 0 else acc
    else:
        q_ref, k_ref, v_ref, g_ref = o_refs
        cs = cos_ref[...]
        sn = sin_ref[...]
        half = RET_DK // 2
        for c, dst, scale in ((0, q_ref, 1.0), (1, k_ref, RET_DK ** -0.5)):
            acc = chunk(c)
            for h in range(RET_HEADS):
                x1 = acc[:, h * RET_DK:h * RET_DK + half] * scale
                x2 = acc[:, h * RET_DK + half:(h + 1) * RET_DK] * scale
                dst[:, h * RET_DK:h * RET_DK + half] = (x1 * cs - x2 * sn).astype(dst.dtype)
                dst[:, h * RET_DK + half:(h + 1) * RET_DK] = (x1 * sn + x2 * cs).astype(dst.dtype)
        for c in range(2):
            v_ref[:, c * tn:(c + 1) * tn] = chunk(2 + c).astype(v_ref.dtype)
            g_ref[:, c * tn:(c + 1) * tn] = chunk(4 + c)


def _proj(x2d, w, cos, sin, *, mode, seq):
    T, K = x2d.shape
    N = w.shape[1]
    tm = 256
    nseq = seq // tm
    row = lambda i: (i, 0)
    if mode == "na":
        widths, dtypes = (N,), (F32,)
    else:
        assert N == 2 * RET_QK_W + 2 * RET_V_W and RET_QK_W == PROJ_COL_CHUNK and RET_V_W == 2 * PROJ_COL_CHUNK
        widths, dtypes = (RET_QK_W, RET_QK_W, RET_V_W, RET_V_W), (BF16, F32, BF16, F32)
    outs = pl.pallas_call(
        functools.partial(_proj_kernel, mode=mode),
        grid=(T // tm,),
        in_specs=[
            pl.BlockSpec((tm, K), row),
            pl.BlockSpec((K, N), lambda i: (0, 0)),
            pl.BlockSpec((tm, RET_DK // 2), lambda i: (i % nseq, 0)),
            pl.BlockSpec((tm, RET_DK // 2), lambda i: (i % nseq, 0)),
        ],
        out_specs=[pl.BlockSpec((tm, n), row) for n in widths],
        out_shape=[jax.ShapeDtypeStruct((T, n), dt) for n, dt in zip(widths, dtypes)],
        compiler_params=_cparams(("parallel",)),
        name="proj_" + mode,
    )(x2d, w.astype(BF16), cos, sin)
    return outs[0] if mode == "na" else outs


def _rotary_tables(seq):
    inv = 1.0 / (ROPE_BASE ** (jnp.arange(0, RET_DK, 2, dtype=F32) / RET_DK))
    ang = jnp.arange(seq, dtype=F32)[:, None] * inv[None, :]
    return jnp.cos(ang), jnp.sin(ang)


def _ret_kernel(cdec_ref, q_ref, k_ref, v_ref, dm_ref, cd_ref, sd_ref, o_ref, state_ref):
    d = pl.program_id(0)
    c = pl.program_id(2)

    @pl.when(c == 0)
    def _():
        state_ref[...] = jnp.zeros_like(state_ref)

    for h in range(RET_HEADS):
        q = q_ref[:, h * RET_DK:(h + 1) * RET_DK].astype(BF16)
        k = k_ref[:, h * RET_DK:(h + 1) * RET_DK]
        v = v_ref[:, h * RET_DV:(h + 1) * RET_DV].astype(BF16)
        scores = lax.dot_general(q, k.astype(BF16), (((1,), (1,)), ((), ())),
                                 preferred_element_type=F32) * dm_ref[h]
        o = jnp.dot(scores.astype(BF16), v, preferred_element_type=F32)
        state = state_ref[h]
        o = o + jnp.dot(q, state.astype(BF16), preferred_element_type=F32) * cd_ref[h]
        ks = (k * sd_ref[h]).astype(BF16)
        state_ref[h] = state * cdec_ref[d * RET_HEADS + h] + lax.dot_general(
            ks, v, (((0,), (0,)), ((), ())), preferred_element_type=F32)
        o_ref[:, h * RET_DV:(h + 1) * RET_DV] = o


def _ret_tables(decay_logit, C):
    log_g = jax.nn.log_sigmoid(decay_logit.astype(F32))
    i = jnp.arange(C, dtype=F32)
    diff = i[:, None] - i[None, :]
    lf = log_g[0][:, None, None]
    lb = log_g[1][:, None, None]
    dm_f = jnp.where((diff >= 0)[None], jnp.exp(lf * jnp.maximum(diff, 0.0)[None]), 0.0)
    dm_b = jnp.where((diff < 0)[None], jnp.exp(lb * jnp.maximum(-diff, 0.0)[None]), 0.0)
    cd_f = jnp.exp(log_g[0][:, None] * (i[None, :] + 1.0))
    cd_b = jnp.exp(log_g[1][:, None] * (C - i[None, :]))
    sd_f = jnp.exp(log_g[0][:, None] * (C - 1.0 - i[None, :]))
    sd_b = jnp.exp(log_g[1][:, None] * i[None, :])
    dm = jnp.stack([dm_f, dm_b]).astype(F32)
    cd = jnp.stack([cd_f, cd_b]).astype(F32)[..., None]
    sd = jnp.stack([sd_f, sd_b]).astype(F32)[..., None]
    cdec = jnp.exp(log_g * C).reshape(-1).astype(F32)
    return dm, cd, sd, cdec


def _retention_core(q, k, v, decay_logit, *, batch, seq):
    T = q.shape[0]
    C = RET_CHUNK
    NC = seq // C
    H = RET_HEADS
    dm, cd, sd, cdec = _ret_tables(decay_logit, C)

    def row(d, b, c):
        return b * NC + c + d * (NC - 1 - 2 * c)

    return pl.pallas_call(
        _ret_kernel,
        grid=(2, batch, NC),
        in_specs=[
            pl.BlockSpec(memory_space=pltpu.SMEM),
            pl.BlockSpec((C, RET_QK_W), lambda d, b, c: (row(d, b, c), 0)),
            pl.BlockSpec((C, RET_QK_W), lambda d, b, c: (row(d, b, c), 0)),
            pl.BlockSpec((C, RET_V_W), lambda d, b, c: (row(d, b, c), 0)),
            pl.BlockSpec((None, H, C, C), lambda d, b, c: (d, 0, 0, 0)),
            pl.BlockSpec((None, H, C, 1), lambda d, b, c: (d, 0, 0, 0)),
            pl.BlockSpec((None, H, C, 1), lambda d, b, c: (d, 0, 0, 0)),
        ],
        out_specs=pl.BlockSpec((None, C, RET_V_W), lambda d, b, c: (d, row(d, b, c), 0)),
        out_shape=jax.ShapeDtypeStruct((2, T, RET_V_W), F32),
        scratch_shapes=[pltpu.VMEM((H, RET_DK, RET_DV), F32)],
        compiler_params=_cparams(("parallel", "parallel", "arbitrary")),
        name="retention",
    )(cdec, q, k, v, dm, cd, sd)


def _layer_norm(u, g, b):
    mu = jnp.mean(u, axis=-1, keepdims=True)
    var = jnp.mean(jnp.square(u - mu), axis=-1, keepdims=True)
    return (u - mu) * lax.rsqrt(var + LN_EPS) * g + b


def _mix_out_kernel(*refs, ret):
    if ret:
        y_ref, g_ref, x_ref, w_ref, lng_ref, lnb_ref, wr_ref, o_ref, aff_ref, wbf_ref = refs
    else:
        y_ref, x_ref, w_ref, lng_ref, lnb_ref, wr_ref, o_ref, aff_ref, wbf_ref = refs

    @pl.when(pl.program_id(0) == 0)
    def _():
        wbf_ref[...] = w_ref[...].astype(BF16)

    if ret:
        y = y_ref[0] + y_ref[1]
        parts = []
        for h in range(RET_HEADS):
            yh = y[:, h * RET_DV:(h + 1) * RET_DV]
            mu = jnp.mean(yh, axis=-1, keepdims=True)
            var = jnp.mean(jnp.square(yh - mu), axis=-1, keepdims=True)
            yn = (yh - mu) * lax.rsqrt(var + GN_EPS)
            gh = g_ref[:, h * RET_DV:(h + 1) * RET_DV]
            parts.append((gh * jax.nn.sigmoid(gh) * yn).astype(BF16))
        z = jnp.concatenate(parts, axis=1)
    else:
        z = y_ref[...].astype(BF16)
    mixed = jnp.dot(z, wbf_ref[...], preferred_element_type=F32)
    xn = _layer_norm(DN_ALPHA * x_ref[...] + mixed, lng_ref[...], lnb_ref[...])
    o_ref[...] = xn
    logits = lax.dot_general(wr_ref[...].astype(BF16), xn.astype(BF16), (((1,), (1,)), ((), ())),
                             preferred_element_type=F32)
    e = jnp.exp(logits - jnp.max(logits, axis=0, keepdims=True))
    aff_ref[...] = e / jnp.sum(e, axis=0, keepdims=True)


def _mix_out(y, gsrc, x2d, w_out, ln_g, ln_b, w_router, *, ret):
    T = x2d.shape[0]
    K = w_out.shape[0]
    tm = 256
    wr_t = w_router.T
    row = lambda i: (i, 0)
    const = lambda i: (0, 0)
    if ret:
        in_specs = [pl.BlockSpec((2, tm, K), lambda i: (0, i, 0)),
                    pl.BlockSpec((tm, K), row)]
        args = [y, gsrc]
    else:
        in_specs = [pl.BlockSpec((tm, K), row)]
        args = [y]
    in_specs += [pl.BlockSpec((tm, D_MODEL), row),
                 pl.BlockSpec((K, D_MODEL), const),
                 pl.BlockSpec((1, D_MODEL), const),
                 pl.BlockSpec((1, D_MODEL), const),
                 pl.BlockSpec((N_EXPERTS, D_MODEL), const)]
    args += [x2d, w_out, ln_g.reshape(1, -1), ln_b.reshape(1, -1), wr_t]
    return pl.pallas_call(
        functools.partial(_mix_out_kernel, ret=ret),
        grid=(T // tm,),
        in_specs=in_specs,
        out_specs=[pl.BlockSpec((tm, D_MODEL), row),
                   pl.BlockSpec((N_EXPERTS, tm), lambda i: (0, i))],
        out_shape=[jax.ShapeDtypeStruct((T, D_MODEL), F32),
                   jax.ShapeDtypeStruct((N_EXPERTS, T), F32)],
        scratch_shapes=[pltpu.VMEM((K, D_MODEL), BF16)],
        compiler_params=_cparams(("arbitrary",)),
        name="mix_out_ret" if ret else "mix_out_na",
    )(*args)


def _na_kernel(q_ref, k0_ref, k1_ref, k2_ref, v0_ref, v1_ref, v2_ref, bias_ref, o_ref,
               ka_ref, kb_ref, va_ref, vb_ref, s_ref, p_ref, *, rows):
    i = pl.program_id(2)
    R = NA_ROWS_PER_STEP
    lo, hi = NA_WIN_STARTS[1], NA_WIN_STARTS[2] + NA_WIN_COLS
    for t, (kr, vr) in enumerate(((k0_ref, v0_ref), (k1_ref, v1_ref), (k2_ref, v2_ref))):
        ka_ref[t * R:(t + 1) * R] = kr[...].astype(BF16)
        kb_ref[t * R:(t + 1) * R] = kr[:, lo:hi, :].astype(BF16)
        va_ref[t * R:(t + 1) * R] = vr[...].astype(BF16)
        vb_ref[t * R:(t + 1) * R] = vr[:, lo:hi, :].astype(BF16)

    nrow = NA_HEAD_GROUP * NA_QGROUP
    width = NA_HEAD_GROUP * NA_HD
    rid = lax.broadcasted_iota(jnp.int32, (nrow, width), 0) // NA_QGROUP
    cid = lax.broadcasted_iota(jnp.int32, (nrow, width), 1) // NA_HD
    head_mask = (rid == cid).astype(F32)
    nkeys = NA_KH * NA_WIN_COLS

    ngroup = len(NA_WIN_STARTS)

    def row_geom(rr):
        r = i * R + rr
        rs = jnp.clip(r - NA_KH // 2, 0, rows - NA_KH)
        return r - rs, rs - (i * R - R)

    def win_at(a_ref, b_ref, j, rl):
        c0 = NA_WIN_STARTS[j]
        if j in (0, 3):
            win = a_ref[pl.ds(rl, NA_KH), c0:c0 + NA_WIN_COLS, :]
        else:
            win = b_ref[pl.ds(rl, NA_KH), c0 - lo:c0 - lo + NA_WIN_COLS, :]
        return win.reshape(nkeys, width)

    def stage_qk(rr, slot):
        delta, rl = row_geom(rr)
        for j in range(ngroup):
            qg = q_ref[rr, j * NA_QGROUP:(j + 1) * NA_QGROUP, :]
            lhs = (jnp.tile(qg, (NA_HEAD_GROUP, 1)) * head_mask).astype(BF16)
            s_ref[slot, j * nrow:(j + 1) * nrow, :] = lax.dot_general(
                lhs, win_at(ka_ref, kb_ref, j, rl), (((1,), (1,)), ((), ())),
                preferred_element_type=F32) + bias_ref[delta, j]

    def stage_softmax(slot):
        s = s_ref[slot]
        e = jnp.exp(s - jnp.max(s, axis=-1, keepdims=True))
        p_ref[slot] = (e * (1.0 / jnp.sum(e, axis=-1, keepdims=True))).astype(BF16)

    def stage_pv(rr, slot):
        _, rl = row_geom(rr)
        for j in range(ngroup):
            pv = jnp.dot(p_ref[slot, j * nrow:(j + 1) * nrow, :], win_at(va_ref, vb_ref, j, rl),
                         preferred_element_type=F32) * head_mask
            out = pv[0:NA_QGROUP]
            for g in range(1, NA_HEAD_GROUP):
                out = out + pv[g * NA_QGROUP:(g + 1) * NA_QGROUP]
            o_ref[rr, j * NA_QGROUP:(j + 1) * NA_QGROUP, :] = out

    stage_qk(0, 0)
    stage_qk(1, 1)
    stage_softmax(0)

    def piped(u, carry):
        t = 1 + 2 * u
        stage_qk(t + 1, 0)
        stage_softmax(1)
        stage_pv(t - 1, 0)
        stage_qk(t + 2, 1)
        stage_softmax(0)
        stage_pv(t, 1)
        return carry

    lax.fori_loop(0, (R - 2) // 2, piped, 0)
    stage_softmax((R - 1) % 2)
    stage_pv(R - 2, (R - 2) % 2)
    stage_pv(R - 1, (R - 1) % 2)


def _na_bias_table(rpb):
    HG = NA_HEADS // NA_HEAD_GROUP
    nj = GRID_W // NA_QGROUP
    delta = np.arange(NA_KH)[:, None]
    a = np.arange(NA_KH)[None, :]
    row_sel = np.eye(2 * NA_KH - 1, dtype=np.float32)[a - delta + (NA_KH - 1)]
    w = (np.arange(nj)[:, None] * NA_QGROUP + np.arange(NA_QGROUP)[None, :])[:, :, None]
    col = np.asarray(NA_WIN_STARTS)[:, None, None] + np.arange(NA_WIN_COLS)[None, None, :]
    cs = np.clip(w - NA_KW // 2, 0, GRID_W - NA_KW)
    valid = (col >= cs) & (col < cs + NA_KW)
    col_off = np.clip(col - w + (NA_KW - 1), 0, 2 * NA_KW - 2)
    col_sel = np.eye(2 * NA_KW - 1, dtype=np.float32)[col_off] * valid[..., None]
    rpb_g = rpb.astype(F32).reshape(HG, NA_HEAD_GROUP, 2 * NA_KH - 1, 2 * NA_KW - 1)
    vals = jnp.einsum("ghrc,dar,jqkc->gdjhqak", rpb_g, row_sel, col_sel, precision=lax.Precision.HIGHEST)
    vals = jnp.where(valid[None, None, :, None, :, None, :], vals, NEG_BIG)
    return vals.reshape(HG, NA_KH, nj, NA_HEAD_GROUP * NA_QGROUP, NA_KH * NA_WIN_COLS)


def _na_core(qkv, rpb, *, batch, seq):
    T = qkv.shape[0]
    rows = seq // GRID_W
    R = NA_ROWS_PER_STEP
    assert rows % R == 0 and rows >= NA_KH
    nb = rows // R
    HG = NA_HEADS // NA_HEAD_GROUP
    width = NA_HEAD_GROUP * NA_HD
    qkv3 = qkv.reshape(T // GRID_W, GRID_W, 3 * D_MODEL)
    bias = _na_bias_table(rpb)
    kcol0 = D_MODEL // width
    vcol0 = 2 * D_MODEL // width
    blk = (R, GRID_W, width)

    def kv_spec(col0, off):
        return pl.BlockSpec(blk, lambda g, b, i: (b * nb + jnp.clip(i + off, 0, nb - 1), 0, col0 + g))

    span = NA_WIN_STARTS[2] + NA_WIN_COLS - NA_WIN_STARTS[1]
    out = pl.pallas_call(
        functools.partial(_na_kernel, rows=rows),
        grid=(HG, batch, nb),
        in_specs=[pl.BlockSpec(blk, lambda g, b, i: (b * nb + i, 0, g)),
                  kv_spec(kcol0, -1), kv_spec(kcol0, 0), kv_spec(kcol0, 1),
                  kv_spec(vcol0, -1), kv_spec(vcol0, 0), kv_spec(vcol0, 1),
                  pl.BlockSpec((None,) + bias.shape[1:], lambda g, b, i: (g, 0, 0, 0, 0))],
        out_specs=pl.BlockSpec(blk, lambda g, b, i: (b * nb + i, 0, g)),
        out_shape=jax.ShapeDtypeStruct((T // GRID_W, GRID_W, D_MODEL), F32),
        scratch_shapes=[pltpu.VMEM((3 * R, GRID_W, width), BF16),
                        pltpu.VMEM((3 * R, span, width), BF16),
                        pltpu.VMEM((3 * R, GRID_W, width), BF16),
                        pltpu.VMEM((3 * R, span, width), BF16),
                        pltpu.VMEM((2, 4 * NA_HEAD_GROUP * NA_QGROUP, NA_KH * NA_WIN_COLS), F32),
                        pltpu.VMEM((2, 4 * NA_HEAD_GROUP * NA_QGROUP, NA_KH * NA_WIN_COLS), BF16)],
        compiler_params=_cparams(("parallel", "parallel", "parallel")),
        name="natten",
    )(qkv3, qkv3, qkv3, qkv3, qkv3, qkv3, qkv3, bias)
    return out.reshape(T, D_MODEL)


def _ffn_kernel(x_ref, gate_ref, wg_ref, wu_ref, wd_ref, o_ref, wgb_ref, wub_ref, wdb_ref):
    @pl.when(pl.program_id(1) == 0)
    def _():
        wgb_ref[...] = wg_ref[...].astype(BF16)
        wub_ref[...] = wu_ref[...].astype(BF16)
        wdb_ref[...] = wd_ref[...].astype(BF16)

    x = x_ref[...].astype(BF16)
    a = jnp.dot(x, wgb_ref[...], preferred_element_type=F32)
    b = jnp.dot(x, wub_ref[...], preferred_element_type=F32)
    hidden = (a * jax.nn.sigmoid(a) * b).astype(BF16)
    o_ref[...] = jnp.dot(hidden, wdb_ref[...], preferred_element_type=F32) * gate_ref[...]


def _expert_ffn(xe, gates, w_gate, w_up, w_down, *, layer):
    E, cap, D = xe.shape
    Fh = w_gate.shape[-1]
    tm = 512
    wspec = lambda a, b: pl.BlockSpec((None, None, a, b), lambda e, i: (layer, e, 0, 0))
    return pl.pallas_call(
        _ffn_kernel,
        grid=(E, cap // tm),
        in_specs=[pl.BlockSpec((None, tm, D), lambda e, i: (e, i, 0)),
                  pl.BlockSpec((None, tm, 1), lambda e, i: (e, i, 0)),
                  wspec(D, Fh), wspec(D, Fh), wspec(Fh, D)],
        out_specs=pl.BlockSpec((None, tm, D), lambda e, i: (e, i, 0)),
        out_shape=jax.ShapeDtypeStruct((E, cap, D), F32),
        scratch_shapes=[pltpu.VMEM((D, Fh), BF16), pltpu.VMEM((D, Fh), BF16), pltpu.VMEM((Fh, D), BF16)],
        compiler_params=_cparams(("parallel", "arbitrary")),
        name="expert_ffn",
    )(xe, gates[..., None], w_gate, w_up, w_down)


def _split_dot(p, rows):
    hi = rows.astype(BF16)
    lo = (rows - hi.astype(F32)).astype(BF16)
    return jnp.dot(p, hi, preferred_element_type=F32) + jnp.dot(p, lo, preferred_element_type=F32)


def _combine_kernel(s0_ref, fast_ref, pos_ref, x_ref, g_ref, b_ref, ye_ref, o_ref,
                    stage_ref, slow_ref, f_ref, sem_ref, slow_sem, *, nblk, cap):
    i = pl.program_id(0)
    slot = i % 2
    E = N_EXPERTS
    TB = COMB_TOKENS

    def first_slot(blk, e):
        return s0_ref[e * (nblk + 1) + blk]

    def window_base(blk, e, win):
        return jnp.minimum((first_slot(blk, e) // 8) * 8, cap - win)

    def window_copy(blk, e, sl):
        src = pl.multiple_of(e * cap + window_base(blk, e, COMB_WIN), 8)
        return pltpu.make_async_copy(ye_ref.at[pl.ds(src, COMB_WIN)],
                                     stage_ref.at[sl, pl.ds(e * COMB_WIN, COMB_WIN)], sem_ref.at[sl])

    @pl.when(i == 0)
    def _():
        slow_ref[...] = jnp.zeros_like(slow_ref)
        for e in range(E):
            window_copy(0, e, 0).start()

    @pl.when(i + 1 < nblk)
    def _():
        for e in range(E):
            window_copy(i + 1, e, 1 - slot).start()

    for e in range(E):
        window_copy(i, e, slot).wait()

    pos = pos_ref[...]
    lane_e = lax.broadcasted_iota(jnp.int32, (TB, E), 1)

    @pl.when(fast_ref[i] == 1)
    def _():
        base = jnp.zeros((1, E), jnp.int32)
        for e in range(E):
            base = jnp.where(lane_e[0:1] == e, window_base(i, e, COMB_WIN), base)
        kidx = jnp.where(pos >= 0, pos - base, -1)
        lane = lax.broadcasted_iota(jnp.int32, (TB, 2 * COMB_WIN), 1)
        cols = []
        for c in range(E // 2):
            k0 = kidx[:, 2 * c:2 * c + 1]
            k1 = kidx[:, 2 * c + 1:2 * c + 2]
            k1 = jnp.where(k1 >= 0, k1 + COMB_WIN, -1)
            cols.append(jnp.where((lane == k0) | (lane == k1), 1.0, 0.0).astype(BF16))
        f_ref[...] = _split_dot(jnp.concatenate(cols, axis=1), stage_ref[slot])

    @pl.when(fast_ref[i] == 0)
    def _():
        lane = lax.broadcasted_iota(jnp.int32, (TB, COMB_SLOW_PAD), 1)

        def body(e, acc):
            base = window_base(i, e, COMB_SLOW_WIN)
            src = pl.multiple_of(e * cap + base, 8)
            cp = pltpu.make_async_copy(ye_ref.at[pl.ds(src, COMB_SLOW_WIN)],
                                       slow_ref.at[pl.ds(0, COMB_SLOW_WIN)], slow_sem)
            cp.start()
            cp.wait()
            col = jnp.max(jnp.where(lane_e == e, pos, -1), axis=1, keepdims=True)
            k = jnp.where(col >= 0, col - base, -1)
            p = jnp.where(lane == k, 1.0, 0.0).astype(BF16)
            return acc + _split_dot(p, slow_ref[...])

        f_ref[...] = lax.fori_loop(0, E, body, jnp.zeros(f_ref.shape, F32))

    o_ref[...] = _layer_norm(DN_ALPHA * x_ref[...] + f_ref[...], g_ref[...], b_ref[...])


def _combine(ye2d, pos_t, s0, fast, x2d, ln_g, ln_b, *, cap):
    T, D = x2d.shape
    E = N_EXPERTS
    TB = COMB_TOKENS
    nblk = T // TB
    assert cap >= COMB_SLOW_WIN and cap % 8 == 0 and T % TB == 0
    row = lambda i, s0, fl: (i, 0)
    const = lambda i, s0, fl: (0, 0)
    return pl.pallas_call(
        functools.partial(_combine_kernel, nblk=nblk, cap=cap),
        grid_spec=pltpu.PrefetchScalarGridSpec(
            num_scalar_prefetch=2,
            grid=(nblk,),
            in_specs=[pl.BlockSpec((TB, E), row), pl.BlockSpec((TB, D), row),
                      pl.BlockSpec((1, D), const), pl.BlockSpec((1, D), const),
                      pl.BlockSpec(memory_space=pl.ANY)],
            out_specs=pl.BlockSpec((TB, D), row),
            scratch_shapes=[pltpu.VMEM((2, E * COMB_WIN, D), F32),
                            pltpu.VMEM((COMB_SLOW_PAD, D), F32),
                            pltpu.VMEM((TB, D), F32),
                            pltpu.SemaphoreType.DMA((2,)),
                            pltpu.SemaphoreType.DMA(())]),
        out_shape=jax.ShapeDtypeStruct((T, D), F32),
        compiler_params=_cparams(("arbitrary",)),
        name="combine_ln",
    )(s0, fast, pos_t, x2d, ln_g.reshape(1, -1), ln_b.reshape(1, -1), ye2d)


def _select_kernel(aff_ref, u_ref, l_ref, pos_ref, off_ref, *, cap):
    E, R, L = aff_ref.shape
    bits = lax.bitcast_convert_type(aff_ref[...], jnp.int32)

    def count(mask):
        ones = jnp.where(mask, 1.0, 0.0)
        return jnp.sum(jnp.sum(ones, axis=2, keepdims=True), axis=1, keepdims=True)

    def step(b, prefix):
        cand = prefix | jnp.left_shift(jnp.int32(1), 30 - b)
        return jnp.where(count(bits >= cand) >= cap, cand, prefix)

    tau = lax.fori_loop(0, 31, step, jnp.zeros((E, 1, 1), jnp.int32))

    def cumsum(mask):
        x2 = jnp.where(mask, 1.0, 0.0).astype(BF16).reshape(E * R, L)
        within = jnp.dot(x2, u_ref[...], preferred_element_type=F32).reshape(E, R, L)
        tot = jnp.broadcast_to(within[:, :, L - 1:L], (E, R, L)).astype(BF16)
        before = jnp.stack([jnp.dot(l_ref[...], tot[e], preferred_element_type=F32) for e in range(E)])
        return within + before, before

    above = bits > tau
    tied = bits == tau
    need = cap - count(above)
    tied_rank, _ = cumsum(tied)
    sel = above | (tied & (tied_rank <= need))
    csum, before = cumsum(sel)
    pos_ref[...] = jnp.where(sel, csum - 1.0, -1.0).astype(jnp.int32)
    off_ref[...] = before.astype(jnp.int32)


def _compact_kernel(off_ref, pos_ref, aff_ref, acc_ref, *, rows, cap):
    n_rows, L = pos_ref.shape
    nchunk = cap // L
    acc_ref[...] = jnp.zeros_like(acc_ref)
    slot_in_chunk = lax.broadcasted_iota(jnp.int32, (L, L), 0)
    field = lax.broadcasted_iota(jnp.int32, (8, L), 0)
    lane = lax.broadcasted_iota(jnp.int32, (8, L), 1).astype(F32)

    def body(n, carry):
        e = n // rows
        r = n - e * rows
        chunk = off_ref[n] // L
        rel = pos_ref[pl.ds(n, 1), :] - chunk * L
        g = aff_ref[pl.ds(n, 1), :]
        g1 = g.astype(BF16).astype(F32)
        g2 = (g - g1).astype(BF16).astype(F32)
        g3 = g - g1 - g2
        vals = jnp.where(field == 0, lane,
                         jnp.where(field == 1, 1.0,
                                   jnp.where(field == 2, g1,
                                             jnp.where(field == 3, g2,
                                                       jnp.where(field == 4, g3, 0.0))))).astype(BF16)
        tok0 = jnp.asarray(r * L, F32)
        for half in range(2):
            hit = jnp.where(rel == slot_in_chunk + half * L, 1.0, 0.0).astype(BF16)
            res = lax.dot_general(vals, hit, (((1,), (1,)), ((), ())), preferred_element_type=F32)
            res = res + jnp.where(field == 0, tok0 * res[1:2, :], 0.0)
            acc_ref[e * nchunk + chunk + half] += res
        return carry

    lax.fori_loop(0, n_rows, body, 0, unroll=16)


def _route(aff_t, cap):
    E, T = aff_t.shape
    L = 128
    R = T // L
    assert T % L == 0 and cap % L == 0 and R % 16 == 0
    u = jnp.asarray(np.arange(L)[:, None] <= np.arange(L)[None, :], BF16)
    low = jnp.asarray(np.arange(R)[None, :] < np.arange(R)[:, None], BF16)
    aff3 = aff_t.reshape(E, R, L)
    full = lambda shape: pl.BlockSpec(shape, lambda i: (0,) * len(shape))
    pos3, before3 = pl.pallas_call(
        functools.partial(_select_kernel, cap=cap),
        grid=(1,),
        in_specs=[full((E, R, L)), full((L, L)), full((R, R))],
        out_specs=[full((E, R, L)), full((E, R, L))],
        out_shape=[jax.ShapeDtypeStruct((E, R, L), jnp.int32)] * 2,
        compiler_params=_cparams(("arbitrary",)),
        name="route_select",
    )(aff3, u, low)
    row_off = before3[:, :, 0]
    nchunk = cap // L
    acc = pl.pallas_call(
        functools.partial(_compact_kernel, rows=R, cap=cap),
        grid_spec=pltpu.PrefetchScalarGridSpec(
            num_scalar_prefetch=1,
            grid=(1,),
            in_specs=[pl.BlockSpec((E * R, L), lambda i, off: (0, 0)),
                      pl.BlockSpec((E * R, L), lambda i, off: (0, 0))],
            out_specs=pl.BlockSpec((E * nchunk + 1, 8, L), lambda i, off: (0, 0, 0))),
        out_shape=jax.ShapeDtypeStruct((E * nchunk + 1, 8, L), F32),
        compiler_params=_cparams(("arbitrary",)),
        name="route_compact",
    )(row_off.reshape(-1), pos3.reshape(E * R, L), aff3.reshape(E * R, L))
    acc = acc[:E * nchunk]
    idx = acc[:, 0, :].astype(jnp.int32).reshape(E, cap)
    gates = ((acc[:, 2, :] + acc[:, 3, :]) + acc[:, 4, :]).reshape(E, cap)
    pos_t = pos3.reshape(E, T).T
    s0 = jnp.concatenate([row_off[:, ::COMB_TOKENS // L], jnp.full((E, 1), cap, jnp.int32)], axis=1)
    fast = jnp.all(s0[:, 1:] - s0[:, :-1] <= COMB_FAST_MAX, axis=0).astype(jnp.int32)
    return idx, gates, pos_t, s0.reshape(-1), fast


def _moe_ln(x2d, aff_t, w_gate, w_up, w_down, ln_g, ln_b, *, layer):
    T, D = x2d.shape
    cap = CAPACITY_FACTOR * T // N_EXPERTS
    idx, gates, pos_t, s0, fast = _route(aff_t, cap)
    xe = x2d.at[idx].get(mode="promise_in_bounds")
    ye = _expert_ffn(xe, gates, w_gate, w_up, w_down, layer=layer)
    return _combine(ye.reshape(N_EXPERTS * cap, D), pos_t, s0, fast, x2d, ln_g, ln_b, cap=cap)


def _trunk(x, ret_w_in, ret_decay, ret_w_out, na_w_in, na_rpb, na_w_out,
           ln_mix_g, ln_mix_b, ln_ffn_g, ln_ffn_b, w_router, w_gate, w_up, w_down):
    B, S, D = x.shape
    x2d = x.reshape(B * S, D)
    cos, sin = _rotary_tables(S)
    for i in range(DEPTH):
        j = i // 2
        if i % 2 == 0:
            q, k, v, g = _proj(x2d, ret_w_in[j], cos, sin, mode="ret", seq=S)
            y = _retention_core(q, k, v, ret_decay[j], batch=B, seq=S)
            x2d, aff_t = _mix_out(y, g, x2d, ret_w_out[j], ln_mix_g[i], ln_mix_b[i], w_router[i], ret=True)
        else:
            qkv = _proj(x2d, na_w_in[j], cos, sin, mode="na", seq=S)
            o = _na_core(qkv, na_rpb[j], batch=B, seq=S)
            x2d, aff_t = _mix_out(o, None, x2d, na_w_out[j], ln_mix_g[i], ln_mix_b[i], w_router[i], ret=False)
        x2d = _moe_ln(x2d, aff_t, w_gate, w_up, w_down, ln_ffn_g[i], ln_ffn_b[i], layer=i)
    return x2d.reshape(B, S, D)


def kernel(x_prompt, x_sample, ret_w_in, ret_decay, ret_w_out, na_w_in, na_rpb, na_w_out, ln_mix_g, ln_mix_b,
           ln_ffn_g, ln_ffn_b, w_router, w_gate, w_up, w_down):
    weights = (ret_w_in, ret_decay, ret_w_out, na_w_in, na_rpb, na_w_out, ln_mix_g, ln_mix_b,
               ln_ffn_g, ln_ffn_b, w_router, w_gate, w_up, w_down)
    return _trunk(x_prompt, *weights), _trunk(x_sample, *weights)
```
